```python
import math
import jax, jax.numpy as jnp
from jax import lax
import numpy as np

D_MODEL = 1024
BATCH = 8
SEQ = 2048
DEPTH = 2

GRID_W = 64
CTX_LEN = 256
EPS = 1e-6
Q_BLOCK = 128
ROPE_THETA = 10000.0
F32 = jnp.float32

POOL_WINDOWS = (2, 4, 8, 16)
POOL_GROUP = D_MODEL // 8
POOL_WIDTH = POOL_GROUP * len(POOL_WINDOWS)
DIFF_HEADS = 4
DIFF_HEAD_DIM = 64
DIFF_V_DIM = 2 * DIFF_HEAD_DIM
DIFF_QK_WIDTH = DIFF_HEADS * 2 * DIFF_HEAD_DIM
DIFF_WIDTH = DIFF_HEADS * DIFF_V_DIM
AB_IN_WIDTH = POOL_WIDTH + 2 * DIFF_QK_WIDTH + DIFF_WIDTH
AB_OUT_WIDTH = POOL_WIDTH + DIFF_WIDTH
MLA_HEADS = 16
MLA_NOPE = 64
MLA_ROPE = 32
MLA_QK = MLA_NOPE + MLA_ROPE
MLA_V = 64
MLA_Q_RANK = 384
MLA_KV_RANK = 256
FFN_HIDDEN = -(-(8 * D_MODEL) // (3 * 256)) * 256
N_AB = (DEPTH + 1) // 2
N_MLA = DEPTH // 2

kernel_name = "hybrid_pool_diffattn_mla_prefix_dit"


def rmsnorm(x, g):
    xf = x.astype(F32)
    y = xf * lax.rsqrt(jnp.mean(xf * xf, axis=-1, keepdims=True) + EPS)
    return (y * g.astype(F32)).astype(x.dtype)


def adaln(cond, w, b):
    m = jax.nn.silu(cond) @ w + b
    return jnp.split(m, 6, axis=-1)


def modulate(h, shift, scale):
    return h * (1.0 + scale) + shift


def axial_rope(rows, rot_dim):
    n_freq = rot_dim // 4
    freqs = ROPE_THETA ** (-jnp.arange(n_freq, dtype=F32) / n_freq)
    row = jnp.repeat(jnp.arange(rows, dtype=F32), GRID_W)
    col = jnp.tile(jnp.arange(GRID_W, dtype=F32), rows)
    ang = jnp.concatenate([row[:, None] * freqs, col[:, None] * freqs], axis=-1)
    return jnp.cos(ang), jnp.sin(ang)


def apply_rope(x, cos, sin):
    shp = (cos.shape[0],) + (1,) * (x.ndim - 3) + (cos.shape[1],)
    cs, sn = cos.reshape(shp), sin.reshape(shp)
    xf = x.astype(F32)
    half = x.shape[-1] // 2
    x1, x2 = xf[..., :half], xf[..., half:]
    return jnp.concatenate([x1 * cs - x2 * sn, x1 * sn + x2 * cs], axis=-1).astype(x.dtype)


def sweep_query_blocks(core, q, *kv):
    b, n = q.shape[:2]
    nb = n // Q_BLOCK
    qb = jnp.moveaxis(q.reshape((b, nb, Q_BLOCK) + q.shape[2:]), 1, 0)
    o = lax.map(lambda qi: core(qi, *kv), qb)
    return jnp.moveaxis(o, 0, 1).reshape((b, n) + o.shape[3:])


def attn_core(q, k, v, scale):
    s = jnp.einsum('bqhd,bkhd->bhqk', q.astype(F32), k.astype(F32)) * scale
    p = jax.nn.softmax(s, axis=-1)
    return jnp.einsum('bhqk,bkhe->bqhe', p, v.astype(F32)).astype(v.dtype)


def diff_attn_core(q, k, v, lam, scale):
    s = jnp.einsum('bqhjd,bkhjd->bhjqk', q.astype(F32), k.astype(F32)) * scale
    p = jax.nn.softmax(s, axis=-1)
    pd = p[:, :, 0] - lam * p[:, :, 1]
    return jnp.einsum('bhqk,bkhe->bqhe', pd, v.astype(F32)).astype(v.dtype)


def multiscale_pool(u, pool_w, pool_scale):
    b, n, _ = u.shape
    uf = u.astype(F32)
    cs = jnp.concatenate([jnp.zeros((b, 1, POOL_WIDTH), F32), jnp.cumsum(uf, axis=1)], axis=1)
    t = jnp.arange(n)
    outs = []
    for g, w in enumerate(POOL_WINDOWS):
        lo = jnp.clip(t - w // 2, 0, n)
        hi = jnp.clip(t - w // 2 + w, 0, n)
        sl = slice(g * POOL_GROUP, (g + 1) * POOL_GROUP)
        csg = cs[:, :, sl]
        mean = (csg[:, hi] - csg[:, lo]) / (hi - lo).astype(F32)[None, :, None]
        outs.append(mean - uf[:, :, sl])
    d = jnp.stack(outs, axis=2)
    y = jnp.einsum('blgc,gce->blge', d, pool_w.astype(F32)).reshape(b, n, POOL_WIDTH)
    return (y * pool_scale.astype(F32)).astype(u.dtype)


def pool_diff_mixer(a_lat, a_ctx, w_in, w_out, pool_w, pool_scale, q_g, k_g,
                    lq1, lk1, lq2, lk2, subln_g, lam_init, cos, sin, need_ctx):
    def project(a):
        b, n, _ = a.shape
        p = a @ w_in
        u, q, k, v = jnp.split(p, [POOL_WIDTH, POOL_WIDTH + DIFF_QK_WIDTH,
                                   POOL_WIDTH + 2 * DIFF_QK_WIDTH], axis=-1)
        q = rmsnorm(q.reshape(b, n, DIFF_HEADS, 2, DIFF_HEAD_DIM), q_g)
        k = rmsnorm(k.reshape(b, n, DIFF_HEADS, 2, DIFF_HEAD_DIM), k_g)
        v = v.reshape(b, n, DIFF_HEADS, DIFF_V_DIM)
        return u, q, k, v

    lam = (jnp.exp(jnp.sum(lq1.astype(F32) * lk1.astype(F32)))
           - jnp.exp(jnp.sum(lq2.astype(F32) * lk2.astype(F32))) + lam_init)
    scale = DIFF_HEAD_DIM ** -0.5

    def finish(u, o):
        b, n, _ = u.shape
        o = rmsnorm(o, subln_g) * (1.0 - lam_init)
        mixed = jnp.concatenate([multiscale_pool(u, pool_w, pool_scale),
                                 o.reshape(b, n, DIFF_WIDTH).astype(u.dtype)], axis=-1)
        return mixed @ w_out

    u_l, q_l, k_l, v_l = project(a_lat)
    u_c, q_c, k_c, v_c = project(a_ctx)
    q_l = apply_rope(q_l, cos, sin)
    k_l = apply_rope(k_l, cos, sin)
    k_all = jnp.concatenate([k_c, k_l], axis=1)
    v_all = jnp.concatenate([v_c, v_l], axis=1)
    o_l = sweep_query_blocks(lambda qi, kk, vv: diff_attn_core(qi, kk, vv, lam, scale), q_l, k_all, v_all)
    out_l = finish(u_l, o_l)
    out_c = None
    if need_ctx:
        out_c = finish(u_c, diff_attn_core(q_c, k_c, v_c, lam, scale))
    return out_l, out_c


def mla_mixer(a_lat, a_ctx, w_dq, q_lat_g, w_uq, w_dkv, kv_lat_g, w_ukv,
              q_g, k_g, w_out, cos, sin, need_ctx):
    def keys_values(a):
        b, n, _ = a.shape
        ckv = a @ w_dkv
        c_kv = rmsnorm(ckv[..., :MLA_KV_RANK], kv_lat_g)
        k_rope = ckv[..., MLA_KV_RANK:]
        kvh = (c_kv @ w_ukv).reshape(b, n, MLA_HEADS, MLA_NOPE + MLA_V)
        k = jnp.concatenate([kvh[..., :MLA_NOPE],
                             jnp.broadcast_to(k_rope[:, :, None, :], (b, n, MLA_HEADS, MLA_ROPE))], axis=-1)
        return rmsnorm(k, k_g), kvh[..., MLA_NOPE:]

    def queries(a):
        b, n, _ = a.shape
        q = (rmsnorm(a @ w_dq, q_lat_g) @ w_uq).reshape(b, n, MLA_HEADS, MLA_QK)
        return rmsnorm(q, q_g)

    def rope_tail(t):
        return jnp.concatenate([t[..., :MLA_NOPE], apply_rope(t[..., MLA_NOPE:], cos, sin)], axis=-1)

    scale = MLA_QK ** -0.5
    k_l, v_l = keys_values(a_lat)
    k_c, v_c = keys_values(a_ctx)
    q_l = rope_tail(queries(a_lat))
    k_l = rope_tail(k_l)
    k_all = jnp.concatenate([k_c, k_l], axis=1)
    v_all = jnp.concatenate([v_c, v_l], axis=1)
    b, n, _ = a_lat.shape
    o_l = sweep_query_blocks(lambda qi, kk, vv: attn_core(qi, kk, vv, scale), q_l, k_all, v_all)
    out_l = o_l.reshape(b, n, MLA_HEADS * MLA_V) @ w_out
    out_c = None
    if need_ctx:
        o_c = attn_core(queries(a_ctx), k_c, v_c, scale)
        out_c = o_c.reshape(a_ctx.shape[0], a_ctx.shape[1], MLA_HEADS * MLA_V) @ w_out
    return out_l, out_c


def swiglu(a, wg, wu, wd):
    return (jax.nn.silu(a @ wg) * (a @ wu)) @ wd


def setup_inputs(seed: int = 0) -> dict:
    key = jax.random.key(seed)
    ks = iter(jax.random.split(key, 48))

    def nrm(shape, scale):
        return jax.random.normal(next(ks), shape, F32) * scale

    def gain(shape):
        return 1.0 + 0.1 * jax.random.normal(next(ks), shape, F32)

    D = D_MODEL
    return {
        "x": nrm((BATCH, SEQ, D), 1.0),
        "c": nrm((BATCH, D), 1.0),
        "ctx": nrm((BATCH, CTX_LEN, D), 1.0),
        "c_ctx": nrm((D,), 1.0),
        "mod_w": nrm((DEPTH, D, 6 * D), 0.5 * D ** -0.5),
        "mod_b": nrm((DEPTH, 6 * D), 0.02),
        "norm_mix_g": gain((DEPTH, D)),
        "norm_ffn_g": gain((DEPTH, D)),
        "ffn_w_gate": nrm((DEPTH, D, FFN_HIDDEN), D ** -0.5),
        "ffn_w_up": nrm((DEPTH, D, FFN_HIDDEN), D ** -0.5),
        "ffn_w_down": nrm((DEPTH, FFN_HIDDEN, D), FFN_HIDDEN ** -0.5),
        "ab_w_in": nrm((N_AB, D, AB_IN_WIDTH), D ** -0.5),
        "ab_w_out": nrm((N_AB, AB_OUT_WIDTH, D), AB_OUT_WIDTH ** -0.5),
        "pool_w": nrm((N_AB, len(POOL_WINDOWS), POOL_GROUP, POOL_GROUP), POOL_GROUP ** -0.5),
        "pool_scale": gain((N_AB, POOL_WIDTH)),
        "diff_q_norm_g": gain((N_AB, DIFF_HEAD_DIM)),
        "diff_k_norm_g": gain((N_AB, DIFF_HEAD_DIM)),
        "diff_lam_q1": nrm((N_AB, DIFF_HEAD_DIM), 0.1),
        "diff_lam_k1": nrm((N_AB, DIFF_HEAD_DIM), 0.1),
        "diff_lam_q2": nrm((N_AB, DIFF_HEAD_DIM), 0.1),
        "diff_lam_k2": nrm((N_AB, DIFF_HEAD_DIM), 0.1),
        "diff_subln_g": gain((N_AB, DIFF_V_DIM)),
        "mla_w_dq": nrm((N_MLA, D, MLA_Q_RANK), D ** -0.5),
        "mla_q_lat_g": gain((N_MLA, MLA_Q_RANK)),
        "mla_w_uq": nrm((N_MLA, MLA_Q_RANK, MLA_HEADS * MLA_QK), MLA_Q_RANK ** -0.5),
        "mla_w_dkv": nrm((N_MLA, D, MLA_KV_RANK + MLA_ROPE), D ** -0.5),
        "mla_kv_lat_g": gain((N_MLA, MLA_KV_RANK)),
        "mla_w_ukv": nrm((N_MLA, MLA_KV_RANK, MLA_HEADS * (MLA_NOPE + MLA_V)), MLA_KV_RANK ** -0.5),
        "mla_q_norm_g": gain((N_MLA, MLA_QK)),
        "mla_k_norm_g": gain((N_MLA, MLA_QK)),
        "mla_w_out": nrm((N_MLA, MLA_HEADS * MLA_V, D), (MLA_HEADS * MLA_V) ** -0.5),
    }


def reference(x, c, ctx, c_ctx, mod_w, mod_b, norm_mix_g, norm_ffn_g,
              ffn_w_gate, ffn_w_up, ffn_w_down,
              ab_w_in, ab_w_out, pool_w, pool_scale, diff_q_norm_g, diff_k_norm_g,
              diff_lam_q1, diff_lam_k1, diff_lam_q2, diff_lam_k2, diff_subln_g,
              mla_w_dq, mla_q_lat_g, mla_w_uq, mla_w_dkv, mla_kv_lat_g, mla_w_ukv,
              mla_q_norm_g, mla_k_norm_g, mla_w_out):
    n_lat = x.shape[1]
    rows = n_lat // GRID_W
    cos_d, sin_d = axial_rope(rows, DIFF_HEAD_DIM)
    cos_m, sin_m = axial_rope(rows, MLA_ROPE)
    h, hc = x, ctx
    for layer in range(DEPTH):
        last = layer == DEPTH - 1
        sh1, sc1, g1, sh2, sc2, g2 = adaln(c, mod_w[layer], mod_b[layer])
        csh1, csc1, cg1, csh2, csc2, cg2 = adaln(c_ctx, mod_w[layer], mod_b[layer])
        a_lat = modulate(rmsnorm(h, norm_mix_g[layer]), sh1[:, None], sc1[:, None])
        a_ctx = modulate(rmsnorm(hc, norm_mix_g[layer]), csh1, csc1)
        if layer % 2 == 0:
            i = layer // 2
            lam_init = 0.8 - 0.6 * math.exp(-0.3 * layer)
            m_lat, m_ctx = pool_diff_mixer(
                a_lat, a_ctx, ab_w_in[i], ab_w_out[i], pool_w[i], pool_scale[i],
                diff_q_norm_g[i], diff_k_norm_g[i], diff_lam_q1[i], diff_lam_k1[i],
                diff_lam_q2[i], diff_lam_k2[i], diff_subln_g[i], lam_init,
                cos_d, sin_d, not last)
        else:
            i = layer // 2
            m_lat, m_ctx = mla_mixer(
                a_lat, a_ctx, mla_w_dq[i], mla_q_lat_g[i], mla_w_uq[i], mla_w_dkv[i],
                mla_kv_lat_g[i], mla_w_ukv[i], mla_q_norm_g[i], mla_k_norm_g[i],
                mla_w_out[i], cos_m, sin_m, not last)
        h = h + g1[:, None] * m_lat
        f_lat = modulate(rmsnorm(h, norm_ffn_g[layer]), sh2[:, None], sc2[:, None])
        h = h + g2[:, None] * swiglu(f_lat, ffn_w_gate[layer], ffn_w_up[layer], ffn_w_down[layer])
        if not last:
            hc = hc + cg1 * m_ctx
            f_ctx = modulate(rmsnorm(hc, norm_ffn_g[layer]), csh2, csc2)
            hc = hc + cg2 * swiglu(f_ctx, ffn_w_gate[layer], ffn_w_up[layer], ffn_w_down[layer])
    return h
```

```python
import functools
import math

import jax
import jax.numpy as jnp
from jax import lax
from jax.experimental import pallas as pl
from jax.experimental.pallas import tpu as pltpu

F32 = jnp.float32
BF16 = jnp.bfloat16

EPS = 1e-6
GRID_W = 64
ROPE_THETA = 10000.0
POOL_WINDOWS = (2, 4, 8, 16)
POOL_GROUP = 128
POOL_WIDTH = POOL_GROUP * len(POOL_WINDOWS)
DIFF_HEADS = 4
DIFF_HEAD_DIM = 64
DIFF_PAIR = 2 * DIFF_HEAD_DIM
DIFF_WIDTH = DIFF_HEADS * DIFF_PAIR
MLA_HEADS = 16
MLA_NOPE = 64
MLA_ROPE = 32
MLA_QK = MLA_NOPE + MLA_ROPE
MLA_V = 64
MLA_KV_RANK = 256
LANES = 128
ROW_TILE = 256
MOD_ROWS = 16
VMEM_LIMIT = 56 * 1024 * 1024

_NT = (((1,), (1,)), ((), ()))


def _dot(a, b):
    return jnp.dot(a, b, preferred_element_type=F32)


def _rms(x, g):
    return x * lax.rsqrt(jnp.mean(x * x, axis=-1, keepdims=True) + EPS) * g


def _params(*sem):
    return pltpu.CompilerParams(dimension_semantics=sem, vmem_limit_bytes=VMEM_LIMIT)


def _const_spec(shape):
    zeros = (0,) * len(shape)
    return pl.BlockSpec(shape, lambda *_: zeros, pipeline_mode=pl.Buffered(1))


def _adaln_kernel(cond_ref, w_ref, b_ref, o_ref):
    c = cond_ref[...]
    a = (c / (1.0 + jnp.exp(-c))).astype(BF16)
    o_ref[0, 0] = _dot(a, w_ref[0].astype(BF16)) + b_ref[0, 0]


def _adaln(cond, mod_w, mod_b):
    depth, d, d6 = mod_w.shape
    n = d6 // d
    out = pl.pallas_call(
        _adaln_kernel,
        grid=(depth, n),
        in_specs=[
            pl.BlockSpec((MOD_ROWS, d), lambda l, j: (0, 0)),
            pl.BlockSpec((1, d, d), lambda l, j: (l, 0, j)),
            pl.BlockSpec((1, 1, 1, d), lambda l, j: (l, j, 0, 0)),
        ],
        out_specs=pl.BlockSpec((1, 1, MOD_ROWS, d), lambda l, j: (l, j, 0, 0)),
        out_shape=jax.ShapeDtypeStruct((depth, n, MOD_ROWS, d), F32),
        compiler_params=_params("arbitrary", "arbitrary"),
        name="adaln",
    )(cond, mod_w, mod_b.reshape(depth, n, 1, d))
    return out.transpose(0, 2, 1, 3)


def _l0_proj_kernel(h_ref, mod_ref, g_ref, w_ref, qg_ref, kg_ref, bd_ref, cos_ref, sa_ref, sb_ref,
                    u_ref, q_ref, k_ref, v_ref):
    m = mod_ref[0]
    a = _rms(h_ref[...], g_ref[...]) * (1.0 + m[1:2]) + m[0:1]
    p = _dot(a.astype(BF16), w_ref[...])
    w = POOL_WIDTH
    cos, sa, sb = cos_ref[...], sa_ref[...], sb_ref[...]

    def norm_rope(z, g):
        ss = _dot((z * z).astype(BF16), bd_ref[...]) * (1.0 / DIFF_HEAD_DIM)
        z = z * lax.rsqrt(ss + EPS) * g
        half = DIFF_HEAD_DIM // 2
        parts = []
        for c in range(z.shape[1] // LANES):
            zc = z[:, c * LANES:(c + 1) * LANES]
            parts.append(zc * cos + pltpu.roll(zc, LANES - half, 1) * sa + pltpu.roll(zc, half, 1) * sb)
        return jnp.concatenate(parts, axis=-1)

    u_ref[...] = p[:, :w].astype(BF16)
    q = norm_rope(p[:, w:w + DIFF_WIDTH], qg_ref[...])
    q_ref[...] = (q * DIFF_HEAD_DIM ** -0.5).astype(BF16)
    k_ref[...] = norm_rope(p[:, w + DIFF_WIDTH:w + 2 * DIFF_WIDTH], kg_ref[...]).astype(BF16)
    v_ref[...] = p[:, w + 2 * DIFF_WIDTH:].astype(BF16)


def _diff_attn_kernel(q_ref, k_ref, v_ref, lq1_ref, lk1_ref, lq2_ref, lk2_ref, sg_ref, o_ref, *, n_ctx, lam_init):
    j = pl.program_id(2)
    lam = (jnp.exp(jnp.sum(lq1_ref[...] * lk1_ref[...], axis=-1, keepdims=True))
           - jnp.exp(jnp.sum(lq2_ref[...] * lk2_ref[...], axis=-1, keepdims=True)) + lam_init)
    q = q_ref[...].astype(F32)
    lane = lax.broadcasted_iota(jnp.int32, q.shape, 1)
    q1 = jnp.where(lane < DIFF_HEAD_DIM, q, 0.0).astype(BF16)
    q2 = jnp.where(lane >= DIFF_HEAD_DIM, q, 0.0).astype(BF16)

    def softmax_parts(qq, kk):
        s = lax.dot_general(qq, kk, _NT, preferred_element_type=F32)
        e = jnp.exp(s - jnp.max(s, axis=-1, keepdims=True))
        return e, jnp.sum(e, axis=-1, keepdims=True)

    def attend(kk, vv):
        e1, l1 = softmax_parts(q1, kk)
        e2, l2 = softmax_parts(q2, kk)
        pd = e1 * (1.0 / l1) - e2 * (lam / l2)
        o = _dot(pd.astype(BF16), vv)
        o_ref[...] = (_rms(o, sg_ref[...]) * (1.0 - lam_init)).astype(BF16)

    @pl.when(j == 0)
    def _():
        attend(k_ref[0, :n_ctx, :], v_ref[0, :n_ctx, :])

    @pl.when(j > 0)
    def _():
        attend(k_ref[0], v_ref[0])


def _l0_out_kernel(u_ref, o_ref, pw_ref, ps_ref, wo_ref, h_ref, mod_ref, out_ref, *, n_ctx, n_tok):
    tm = h_ref.shape[0]
    win = 2 * tm
    j = pl.program_id(1)
    t0 = j * tm
    seg_lo = jnp.where(j == 0, 0, n_ctx)
    seg_hi = jnp.where(j == 0, n_ctx, n_tok)
    ws = jnp.where(j == 0, 0, jnp.clip(t0 - tm // 2, n_ctx, n_tok - win))
    uw = u_ref[0, pl.ds(pl.multiple_of(ws, LANES), win), :]
    ut = u_ref[0, pl.ds(pl.multiple_of(t0, tm), tm), :].astype(F32)
    row = t0 + lax.broadcasted_iota(jnp.int32, (tm, 1), 0)
    col = ws + lax.broadcasted_iota(jnp.int32, (tm, win), 1)
    ps = ps_ref[...]
    ys = []
    for g, w in enumerate(POOL_WINDOWS):
        sl = slice(g * POOL_GROUP, (g + 1) * POOL_GROUP)
        lo = jnp.maximum(row - w // 2, seg_lo)
        hi = jnp.minimum(row - w // 2 + w, seg_hi)
        band = jnp.where(col >= lo, jnp.where(col < hi, 1.0, 0.0), 0.0).astype(BF16)
        mean = _dot(band, uw[:, sl]) / (hi - lo).astype(F32)
        y = _dot((mean - ut[:, sl]).astype(BF16), pw_ref[g]) * ps[:, sl]
        ys.append(y.astype(BF16))
    y = jnp.concatenate(ys, axis=-1)
    mixed = _dot(y, wo_ref[:POOL_WIDTH, :]) + _dot(o_ref[...], wo_ref[POOL_WIDTH:, :])
    out_ref[...] = h_ref[...] + mod_ref[0][2:3] * mixed


def _ffn_kernel(h_ref, mod_ref, g_ref, wg_ref, wu_ref, wd_ref, out_ref):
    m = mod_ref[0]
    h = h_ref[...]
    a = (_rms(h, g_ref[...]) * (1.0 + m[4:5]) + m[3:4]).astype(BF16)
    gate = _dot(a, wg_ref[...])
    up = _dot(a, wu_ref[...])
    hid = (gate / (1.0 + jnp.exp(-gate)) * up).astype(BF16)
    out_ref[...] = h + m[5:6] * _dot(hid, wd_ref[...])


def _l1_proj_kernel(h_ref, mod_ref, g_ref, wdq_ref, qlg_ref, wuq_ref, wdkv_ref, kvg_ref, wukv_ref,
                    qg_ref, kg_ref, cos_ref, sa_ref, sb_ref, q_ref, k_ref, v_ref):
    m = mod_ref[0]
    a = (_rms(h_ref[...], g_ref[...]) * (1.0 + m[1:2]) + m[0:1]).astype(BF16)
    cos, sa, sb = cos_ref[...], sa_ref[...], sb_ref[...]
    half = MLA_ROPE // 2

    def rope(z):
        return z * cos + pltpu.roll(z, LANES - half, 1) * sa + pltpu.roll(z, half, 1) * sb

    def head_rsqrt(z):
        return lax.rsqrt(jnp.sum(z * z, axis=-1, keepdims=True) * (1.0 / MLA_QK) + EPS)

    cq = _rms(_dot(a, wdq_ref[...]), qlg_ref[...]).astype(BF16)
    qf = _dot(cq, wuq_ref[...])
    qg = qg_ref[...] * MLA_QK ** -0.5
    for hd in range(MLA_HEADS):
        sl = slice(hd * LANES, (hd + 1) * LANES)
        z = qf[:, sl]
        q_ref[:, sl] = rope(z * head_rsqrt(z) * qg).astype(BF16)

    ckv = _dot(a, wdkv_ref[...])
    kr = ckv[:, MLA_KV_RANK:]
    ckvn = _rms(ckv[:, :MLA_KV_RANK], kvg_ref[...]).astype(BF16)
    kv = _dot(ckvn, wukv_ref[...])
    kg = kg_ref[...]
    krr = rope(kr * kg)
    for hd in range(MLA_HEADS):
        sl = slice(hd * LANES, (hd + 1) * LANES)
        kn = kv[:, sl]
        k_ref[:, sl] = (head_rsqrt(kn + kr) * (kn * kg + krr)).astype(BF16)
    v_ref[...] = kv[:, MLA_HEADS * LANES:].astype(BF16)


def _mla_attn_kernel(q_ref, k_ref, v_ref, o_ref):
    vv = v_ref[0]
    outs = []
    for i in range(2):
        sl = slice(i * LANES, (i + 1) * LANES)
        s = lax.dot_general(q_ref[:, sl], k_ref[0, :, sl], _NT, preferred_element_type=F32)
        e = jnp.exp(s - jnp.max(s, axis=-1, keepdims=True))
        l = jnp.sum(e, axis=-1, keepdims=True)
        outs.append(_dot(e.astype(BF16), vv) * (1.0 / l))
    lane = lax.broadcasted_iota(jnp.int32, outs[0].shape, 1)
    o_ref[...] = jnp.where(lane < MLA_V, outs[0], outs[1]).astype(BF16)


def _l1_out_kernel(o_ref, wo_ref, h_ref, mod_ref, out_ref):
    out_ref[...] = h_ref[...] + mod_ref[0][2:3] * _dot(o_ref[...], wo_ref[...])


def _rope_tables(n_ctx, n_lat, rot_dim, first_lane):
    n_freq = rot_dim // 4
    half = rot_dim // 2
    freqs = ROPE_THETA ** (-jnp.arange(n_freq, dtype=F32) / n_freq)
    rows = n_lat // GRID_W
    row = jnp.repeat(jnp.arange(rows, dtype=F32), GRID_W)
    col = jnp.tile(jnp.arange(GRID_W, dtype=F32), rows)
    ang = jnp.concatenate([row[:, None] * freqs, col[:, None] * freqs], axis=-1)
    ang = jnp.concatenate([jnp.zeros((n_ctx, half), F32), ang], axis=0)
    cos_h, sin_h = jnp.cos(ang), jnp.sin(ang)
    zero = jnp.zeros_like(sin_h)
    n_rep = (LANES - first_lane) // rot_dim if first_lane == 0 else 1
    cos = jnp.concatenate([cos_h, cos_h] * n_rep, axis=-1)
    sa = jnp.concatenate([-sin_h, zero] * n_rep, axis=-1)
    sb = jnp.concatenate([zero, sin_h] * n_rep, axis=-1)
    n_rows = n_ctx + n_lat
    pad_lo = first_lane
    pad_hi = LANES - first_lane - cos.shape[1]
    cos = jnp.concatenate([jnp.ones((n_rows, pad_lo), F32), cos, jnp.ones((n_rows, pad_hi), F32)], axis=-1)
    sa = jnp.pad(sa, ((0, 0), (pad_lo, pad_hi)))
    sb = jnp.pad(sb, ((0, 0), (pad_lo, pad_hi)))
    return cos, sa, sb


def _pad_heads(w, width):
    k = w.shape[0]
    w = w.reshape(k, MLA_HEADS, width)
    return jnp.pad(w, ((0, 0), (0, 0), (0, LANES - width))).reshape(k, MLA_HEADS * LANES)


def kernel(x, c, ctx, c_ctx, mod_w, mod_b, norm_mix_g, norm_ffn_g, ffn_w_gate, ffn_w_up, ffn_w_down, ab_w_in, ab_w_out, pool_w, pool_scale, diff_q_norm_g, diff_k_norm_g, diff_lam_q1, diff_lam_k1, diff_lam_q2, diff_lam_k2, diff_subln_g, mla_w_dq, mla_q_lat_g, mla_w_uq, mla_w_dkv, mla_kv_lat_g, mla_w_ukv, mla_q_norm_g, mla_k_norm_g, mla_w_out):
    n_batch, n_lat, d = x.shape
    n_ctx = ctx.shape[1]
    n_tok = n_ctx + n_lat
    tm = ROW_TILE
    assert n_ctx == tm and n_lat % tm == 0 and n_lat >= 2 * tm and n_batch < MOD_ROWS
    assert mod_w.shape[0] == 2 and ab_w_in.shape[0] == 1 and mla_w_dq.shape[0] == 1
    tpb = n_tok // tm
    lpb = n_lat // tm
    n_rows = n_batch * n_tok
    ffn_hidden = ffn_w_gate.shape[-1]
    ctx_mod = n_batch

    cond = jnp.concatenate([c, c_ctx[None, :], jnp.zeros((MOD_ROWS - n_batch - 1, d), F32)], axis=0)
    mods = _adaln(cond, mod_w, mod_b)
    h = jnp.concatenate([ctx, x], axis=1).reshape(n_rows, d)

    def all_mod(t):
        return (jnp.where(t % tpb == 0, ctx_mod, t // tpb), 0, 0)

    def lat_row(t):
        return (t // lpb) * tpb + 1 + t % lpb

    row_spec = lambda w: pl.BlockSpec((tm, w), lambda t: (t, 0))
    mod_spec = pl.BlockSpec((1, 6, d), all_mod)
    vec = lambda v: v.reshape(1, -1)

    lam_init = 0.8 - 0.6 * math.exp(-0.3 * 0)
    cos, sa, sb = _rope_tables(n_ctx, n_lat, DIFF_HEAD_DIM, 0)
    grp = jnp.arange(DIFF_WIDTH) // DIFF_HEAD_DIM
    block_diag = (grp[:, None] == grp[None, :]).astype(BF16)
    tab_spec = pl.BlockSpec((tm, LANES), lambda t: (t % tpb, 0))
    qkv_shape = jax.ShapeDtypeStruct((n_rows, DIFF_WIDTH), BF16)
    u, q, k, v = pl.pallas_call(
        _l0_proj_kernel,
        grid=(n_rows // tm,),
        in_specs=[row_spec(d), mod_spec, _const_spec((1, d)), _const_spec(ab_w_in.shape[1:]),
                  _const_spec((1, DIFF_WIDTH)), _const_spec((1, DIFF_WIDTH)),
                  _const_spec((DIFF_WIDTH, DIFF_WIDTH)), tab_spec, tab_spec, tab_spec],
        out_specs=[row_spec(POOL_WIDTH), row_spec(DIFF_WIDTH), row_spec(DIFF_WIDTH), row_spec(DIFF_WIDTH)],
        out_shape=[jax.ShapeDtypeStruct((n_rows, POOL_WIDTH), BF16), qkv_shape, qkv_shape, qkv_shape],
        compiler_params=_params("arbitrary"),
        name="l0_proj",
    )(h, mods[0], vec(norm_mix_g[0]), ab_w_in[0].astype(BF16),
      vec(jnp.tile(diff_q_norm_g[0], 2 * DIFF_HEADS)), vec(jnp.tile(diff_k_norm_g[0], 2 * DIFF_HEADS)),
      block_diag, cos, sa, sb)

    kv_spec = pl.BlockSpec((1, n_tok, DIFF_PAIR), lambda b, hd, j: (b, 0, hd))
    qo_spec = pl.BlockSpec((tm, DIFF_PAIR), lambda b, hd, j: (b * tpb + j, hd))
    lam_spec = pl.BlockSpec((1, DIFF_HEAD_DIM), lambda b, hd, j: (0, 0))
    o = pl.pallas_call(
        functools.partial(_diff_attn_kernel, n_ctx=n_ctx, lam_init=lam_init),
        grid=(n_batch, DIFF_HEADS, tpb),
        in_specs=[qo_spec, kv_spec, kv_spec, lam_spec, lam_spec, lam_spec, lam_spec,
                  pl.BlockSpec((1, DIFF_PAIR), lambda b, hd, j: (0, 0))],
        out_specs=qo_spec,
        out_shape=qkv_shape,
        compiler_params=_params("arbitrary", "arbitrary", "arbitrary"),
        name="diff_attn",
    )(q, k.reshape(n_batch, n_tok, DIFF_WIDTH), v.reshape(n_batch, n_tok, DIFF_WIDTH),
      vec(diff_lam_q1[0]), vec(diff_lam_k1[0]), vec(diff_lam_q2[0]), vec(diff_lam_k2[0]), vec(diff_subln_g[0]))

    bt_row = lambda w: pl.BlockSpec((tm, w), lambda b, j: (b * tpb + j, 0))
    h = pl.pallas_call(
        functools.partial(_l0_out_kernel, n_ctx=n_ctx, n_tok=n_tok),
        grid=(n_batch, tpb),
        in_specs=[pl.BlockSpec((1, n_tok, POOL_WIDTH), lambda b, j: (b, 0, 0)), bt_row(DIFF_WIDTH),
                  _const_spec(pool_w.shape[1:]), _const_spec((1, POOL_WIDTH)), _const_spec(ab_w_out.shape[1:]),
                  bt_row(d), pl.BlockSpec((1, 6, d), lambda b, j: (jnp.where(j == 0, ctx_mod, b), 0, 0))],
        out_specs=bt_row(d),
        out_shape=jax.ShapeDtypeStruct((n_rows, d), F32),
        compiler_params=_params("arbitrary", "arbitrary"),
        name="l0_out",
    )(u.reshape(n_batch, n_tok, POOL_WIDTH), o, pool_w[0].astype(BF16), vec(pool_scale[0]),
      ab_w_out[0].astype(BF16), h, mods[0])

    ffn_w_specs = [_const_spec((1, d)), _const_spec((d, ffn_hidden)), _const_spec((d, ffn_hidden)),
                   _const_spec((ffn_hidden, d))]
    h = pl.pallas_call(
        _ffn_kernel,
        grid=(n_rows // tm,),
        in_specs=[row_spec(d), mod_spec] + ffn_w_specs,
        out_specs=row_spec(d),
        out_shape=jax.ShapeDtypeStruct((n_rows, d), F32),
        compiler_params=_params("arbitrary"),
        name="ffn0",
    )(h, mods[0], vec(norm_ffn_g[0]), ffn_w_gate[0].astype(BF16), ffn_w_up[0].astype(BF16),
      ffn_w_down[0].astype(BF16))

    cos, sa, sb = _rope_tables(n_ctx, n_lat, MLA_ROPE, MLA_NOPE)
    w_uq = _pad_heads(mla_w_uq[0], MLA_QK).astype(BF16)
    w_dkv = jnp.pad(mla_w_dkv[0], ((0, 0), (0, 0)))
    w_dkv = jnp.concatenate([
        w_dkv[:, :MLA_KV_RANK], jnp.zeros((d, MLA_NOPE), F32), w_dkv[:, MLA_KV_RANK:],
        jnp.zeros((d, LANES - MLA_QK), F32)], axis=-1).astype(BF16)
    w_ukv = mla_w_ukv[0].reshape(MLA_KV_RANK, MLA_HEADS, MLA_NOPE + MLA_V)
    w_ukv = jnp.concatenate([
        _pad_heads(w_ukv[:, :, :MLA_NOPE].reshape(MLA_KV_RANK, -1), MLA_NOPE),
        w_ukv[:, :, MLA_NOPE:].reshape(MLA_KV_RANK, -1)], axis=-1).astype(BF16)
    pad_gain = lambda g: vec(jnp.pad(g, (0, LANES - MLA_QK)))
    hq = MLA_HEADS * LANES
    q, k, v = pl.pallas_call(
        _l1_proj_kernel,
        grid=(n_rows // tm,),
        in_specs=[row_spec(d), mod_spec, _const_spec((1, d)), _const_spec(mla_w_dq.shape[1:]),
                  _const_spec((1, mla_w_dq.shape[2])), _const_spec(w_uq.shape), _const_spec(w_dkv.shape),
                  _const_spec((1, MLA_KV_RANK)), _const_spec(w_ukv.shape), _const_spec((1, LANES)),
                  _const_spec((1, LANES)), tab_spec, tab_spec, tab_spec],
        out_specs=[row_spec(hq), row_spec(hq), row_spec(MLA_HEADS * MLA_V)],
        out_shape=[jax.ShapeDtypeStruct((n_rows, hq), BF16), jax.ShapeDtypeStruct((n_rows, hq), BF16),
                   jax.ShapeDtypeStruct((n_rows, MLA_HEADS * MLA_V), BF16)],
        compiler_params=_params("arbitrary"),
        name="l1_proj",
    )(h, mods[1], vec(norm_mix_g[1]), mla_w_dq[0].astype(BF16), vec(mla_q_lat_g[0]), w_uq, w_dkv,
      vec(mla_kv_lat_g[0]), w_ukv, pad_gain(mla_q_norm_g[0]), pad_gain(mla_k_norm_g[0]), cos, sa, sb)

    n_lat_rows = n_batch * n_lat
    o = pl.pallas_call(
        _mla_attn_kernel,
        grid=(n_batch, MLA_HEADS // 2, lpb),
        in_specs=[pl.BlockSpec((tm, 2 * LANES), lambda b, hp, j: (b * tpb + 1 + j, hp)),
                  pl.BlockSpec((1, n_tok, 2 * LANES), lambda b, hp, j: (b, 0, hp)),
                  pl.BlockSpec((1, n_tok, 2 * MLA_V), lambda b, hp, j: (b, 0, hp))],
        out_specs=pl.BlockSpec((tm, 2 * MLA_V), lambda b, hp, j: (b * lpb + j, hp)),
        out_shape=jax.ShapeDtypeStruct((n_lat_rows, MLA_HEADS * MLA_V), BF16),
        compiler_params=_params("arbitrary", "arbitrary", "arbitrary"),
        name="mla_attn",
    )(q, k.reshape(n_batch, n_tok, hq), v.reshape(n_batch, n_tok, MLA_HEADS * MLA_V))

    lat_mod_spec = pl.BlockSpec((1, 6, d), lambda t: (t // lpb, 0, 0))
    h = pl.pallas_call(
        _l1_out_kernel,
        grid=(n_lat_rows // tm,),
        in_specs=[row_spec(MLA_HEADS * MLA_V), _const_spec(mla_w_out.shape[1:]),
                  pl.BlockSpec((tm, d), lambda t: (lat_row(t), 0)), lat_mod_spec],
        out_specs=row_spec(d),
        out_shape=jax.ShapeDtypeStruct((n_lat_rows, d), F32),
        compiler_params=_params("arbitrary"),
        name="l1_out",
    )(o, mla_w_out[0].astype(BF16), h, mods[1])

    h = pl.pallas_call(
        _ffn_kernel,
        grid=(n_lat_rows // tm,),
        in_specs=[row_spec(d), lat_mod_spec] + ffn_w_specs,
        out_specs=row_spec(d),
        out_shape=jax.ShapeDtypeStruct((n_lat_rows, d), F32),
        compiler_params=_params("arbitrary"),
        name="ffn1",
    )(h, mods[1], vec(norm_ffn_g[1]), ffn_w_gate[1].astype(BF16), ffn_w_up[1].astype(BF16),
      ffn_w_down[1].astype(BF16))
    return h.reshape(n_batch, n_lat, d)
```

```python
import functools
import math

import jax
import jax.numpy as jnp
from jax import lax
from jax.experimental import pallas as pl
from jax.experimental.pallas import tpu as pltpu

F32 = jnp.float32
BF16 = jnp.bfloat16

EPS = 1e-6
GRID_W = 64
ROPE_THETA = 10000.0
POOL_WINDOWS = (2, 4, 8, 16)
POOL_GROUP = 128
POOL_WIDTH = POOL_GROUP * len(POOL_WINDOWS)
DIFF_HEADS = 4
DIFF_HEAD_DIM = 64
DIFF_PAIR = 2 * DIFF_HEAD_DIM
DIFF_WIDTH = DIFF_HEADS * DIFF_PAIR
MLA_HEADS = 16
MLA_NOPE = 64
MLA_ROPE = 32
MLA_QK = MLA_NOPE + MLA_ROPE
MLA_V = 64
MLA_KV_RANK = 256
LANES = 128
ROW_TILE = 256
MOD_ROWS = 16
ONES_ROWS = 16
LOG2E = math.log2(math.e)
DIFF_HEADS_PER_STEP = 4
MLA_HEADS_PER_STEP = 8
SCORE_LOOKAHEAD = 3
VMEM_LIMIT = 56 * 1024 * 1024

_NT = (((1,), (1,)), ((), ()))


def _dot(a, b):
    return jnp.dot(a, b, preferred_element_type=F32)


def _rms(x, g):
    return x * lax.rsqrt(jnp.mean(x * x, axis=-1, keepdims=True) + EPS) * g


def _params(*sem):
    return pltpu.CompilerParams(dimension_semantics=sem, vmem_limit_bytes=VMEM_LIMIT)


def _const_spec(shape):
    zeros = (0,) * len(shape)
    return pl.BlockSpec(shape, lambda *_: zeros, pipeline_mode=pl.Buffered(1))


def _adaln_kernel(cond_ref, w_ref, b_ref, o_ref):
    c = cond_ref[...]
    a = (c / (1.0 + jnp.exp(-c))).astype(BF16)
    o_ref[0, 0] = _dot(a, w_ref[0].astype(BF16)) + b_ref[0, 0]


def _adaln(cond, mod_w, mod_b):
    depth, d, d6 = mod_w.shape
    n = d6 // d
    out = pl.pallas_call(
        _adaln_kernel,
        grid=(depth, n),
        in_specs=[
            pl.BlockSpec((MOD_ROWS, d), lambda l, j: (0, 0)),
            pl.BlockSpec((1, d, d), lambda l, j: (l, 0, j)),
            pl.BlockSpec((1, 1, 1, d), lambda l, j: (l, j, 0, 0)),
        ],
        out_specs=pl.BlockSpec((1, 1, MOD_ROWS, d), lambda l, j: (l, j, 0, 0)),
        out_shape=jax.ShapeDtypeStruct((depth, n, MOD_ROWS, d), F32),
        compiler_params=_params("arbitrary", "arbitrary"),
        name="adaln",
    )(cond, mod_w, mod_b.reshape(depth, n, 1, d))
    return out.transpose(0, 2, 1, 3)


def _l0_proj_kernel(h_ref, mod_ref, g_ref, w_ref, qg_ref, kg_ref, bd_ref, cos_ref, sa_ref, sb_ref,
                    u_ref, q_ref, k_ref, vt_ref):
    m = mod_ref[0]
    a = _rms(h_ref[...], g_ref[...]) * (1.0 + m[1:2]) + m[0:1]
    p = _dot(a.astype(BF16), w_ref[...])
    w = POOL_WIDTH
    cos, sa, sb = cos_ref[...], sa_ref[...], sb_ref[...]

    def norm_rope(z, g):
        ss = _dot((z * z).astype(BF16), bd_ref[...]) * (1.0 / DIFF_HEAD_DIM)
        z = z * lax.rsqrt(ss + EPS) * g
        half = DIFF_HEAD_DIM // 2
        parts = []
        for c in range(z.shape[1] // LANES):
            zc = z[:, c * LANES:(c + 1) * LANES]
            parts.append(zc * cos + pltpu.roll(zc, LANES - half, 1) * sa + pltpu.roll(zc, half, 1) * sb)
        return jnp.concatenate(parts, axis=-1)

    u_ref[...] = p[:, :w].astype(BF16)
    q = norm_rope(p[:, w:w + DIFF_WIDTH], qg_ref[...])
    q_ref[...] = (q * (DIFF_HEAD_DIM ** -0.5 * LOG2E)).astype(BF16)
    k_ref[...] = norm_rope(p[:, w + DIFF_WIDTH:w + 2 * DIFF_WIDTH], kg_ref[...]).astype(BF16)
    vt_ref[0] = p[:, w + 2 * DIFF_WIDTH:].T.astype(BF16)


def _attend_t(problems):
    def scores(p):
        return lax.dot_general(p[0], p[1], _NT, preferred_element_type=F32)

    def finish(st, vt):
        n, dv = st.shape[0], vt.shape[0]
        e = jnp.exp2(st - jnp.max(st, axis=0, keepdims=True)).astype(BF16)
        r = _dot(jnp.concatenate([vt, jnp.ones((ONES_ROWS, n), BF16)], axis=0), e)
        return r[:dv], r[dv:dv + 1]

    outs, sts = [], []
    for i, p in enumerate(problems):
        while len(sts) < min(i + 1 + SCORE_LOOKAHEAD, len(problems)):
            sts.append(scores(problems[len(sts)]))
        outs.append(finish(sts[i], p[2]))
    return outs


def _diff_attn_kernel(q_ref, k_ref, vt_ref, lq1_ref, lk1_ref, lq2_ref, lk2_ref, sg_ref, o_ref, *, n_ctx, lam_init):
    j = pl.program_id(2)
    lam = (jnp.exp(jnp.sum(lq1_ref[...] * lk1_ref[...], axis=-1, keepdims=True))
           - jnp.exp(jnp.sum(lq2_ref[...] * lk2_ref[...], axis=-1, keepdims=True)) + lam_init)
    n_heads = q_ref.shape[1] // DIFF_PAIR
    lane = lax.broadcasted_iota(jnp.int32, (q_ref.shape[0], DIFF_PAIR), 1)

    def attend(n_keys):
        problems = []
        for hd in range(n_heads):
            sl = slice(hd * DIFF_PAIR, (hd + 1) * DIFF_PAIR)
            q = q_ref[:, sl].astype(F32)
            kk, vt = k_ref[0, :n_keys, sl], vt_ref[0, sl, :n_keys]
            problems.append((kk, jnp.where(lane < DIFF_HEAD_DIM, q, 0.0).astype(BF16), vt))
            problems.append((kk, jnp.where(lane >= DIFF_HEAD_DIM, q, 0.0).astype(BF16), vt))
        outs = _attend_t(problems)
        for hd in range(n_heads):
            (o1, l1), (o2, l2) = outs[2 * hd], outs[2 * hd + 1]
            o = (o1 * (1.0 / l1) - o2 * (lam / l2)).T
            o_ref[:, hd * DIFF_PAIR:(hd + 1) * DIFF_PAIR] = (_rms(o, sg_ref[...]) * (1.0 - lam_init)).astype(BF16)

    @pl.when(j == 0)
    def _():
        attend(n_ctx)

    @pl.when(j > 0)
    def _():
        attend(k_ref.shape[1])


def _l0_out_kernel(u_ref, o_ref, pw_ref, ps_ref, wo_ref, h_ref, mod_ref, out_ref, *, n_ctx, n_tok):
    tm = h_ref.shape[0]
    win = 2 * tm
    j = pl.program_id(1)
    t0 = j * tm
    seg_lo = jnp.where(j == 0, 0, n_ctx)
    seg_hi = jnp.where(j == 0, n_ctx, n_tok)
    ws = jnp.where(j == 0, 0, jnp.clip(t0 - tm // 2, n_ctx, n_tok - win))
    uw = u_ref[0, pl.ds(pl.multiple_of(ws, LANES), win), :]
    ut = u_ref[0, pl.ds(pl.multiple_of(t0, tm), tm), :].astype(F32)
    row = t0 + lax.broadcasted_iota(jnp.int32, (tm, 1), 0)
    col = ws + lax.broadcasted_iota(jnp.int32, (tm, win), 1)
    ps = ps_ref[...]
    ys = []
    for g, w in enumerate(POOL_WINDOWS):
        sl = slice(g * POOL_GROUP, (g + 1) * POOL_GROUP)
        lo = jnp.maximum(row - w // 2, seg_lo)
        hi = jnp.minimum(row - w // 2 + w, seg_hi)
        band = jnp.where(col >= lo, jnp.where(col < hi, 1.0, 0.0), 0.0).astype(BF16)
        mean = _dot(band, uw[:, sl]) / (hi - lo).astype(F32)
        y = _dot((mean - ut[:, sl]).astype(BF16), pw_ref[g]) * ps[:, sl]
        ys.append(y.astype(BF16))
    y = jnp.concatenate(ys, axis=-1)
    mixed = _dot(y, wo_ref[:POOL_WIDTH, :]) + _dot(o_ref[...], wo_ref[POOL_WIDTH:, :])
    out_ref[...] = h_ref[...] + mod_ref[0][2:3] * mixed


def _ffn_kernel(h_ref, mod_ref, g_ref, wg_ref, wu_ref, wd_ref, out_ref):
    m = mod_ref[0]
    h = h_ref[...]
    a = (_rms(h, g_ref[...]) * (1.0 + m[4:5]) + m[3:4]).astype(BF16)
    gate = _dot(a, wg_ref[...])
    up = _dot(a, wu_ref[...])
    hid = (gate / (1.0 + jnp.exp(-gate)) * up).astype(BF16)
    out_ref[...] = h + m[5:6] * _dot(hid, wd_ref[...])


def _l1_proj_kernel(h_ref, mod_ref, g_ref, wdq_ref, qlg_ref, wuq_ref, wdkv_ref, kvg_ref, wukv_ref,
                    qg_ref, kg_ref, cos_ref, sa_ref, sb_ref, q_ref, k_ref, vt_ref):
    m = mod_ref[0]
    a = (_rms(h_ref[...], g_ref[...]) * (1.0 + m[1:2]) + m[0:1]).astype(BF16)
    cos, sa, sb = cos_ref[...], sa_ref[...], sb_ref[...]
    half = MLA_ROPE // 2

    def rope(z):
        return z * cos + pltpu.roll(z, LANES - half, 1) * sa + pltpu.roll(z, half, 1) * sb

    def head_rsqrt(z):
        return lax.rsqrt(jnp.sum(z * z, axis=-1, keepdims=True) * (1.0 / MLA_QK) + EPS)

    cq = _rms(_dot(a, wdq_ref[...]), qlg_ref[...]).astype(BF16)
    qf = _dot(cq, wuq_ref[...])
    qg = qg_ref[...] * (MLA_QK ** -0.5 * LOG2E)
    for hd in range(MLA_HEADS):
        sl = slice(hd * LANES, (hd + 1) * LANES)
        z = qf[:, sl]
        q_ref[:, sl] = rope(z * head_rsqrt(z) * qg).astype(BF16)

    ckv = _dot(a, wdkv_ref[...])
    kr = ckv[:, MLA_KV_RANK:]
    ckvn = _rms(ckv[:, :MLA_KV_RANK], kvg_ref[...]).astype(BF16)
    kv = _dot(ckvn, wukv_ref[...])
    kg = kg_ref[...]
    krr = rope(kr * kg)
    for hd in range(MLA_HEADS):
        sl = slice(hd * LANES, (hd + 1) * LANES)
        kn = kv[:, sl]
        k_ref[:, sl] = (head_rsqrt(kn + kr) * (kn * kg + krr)).astype(BF16)
    vt_ref[0] = kv[:, MLA_HEADS * LANES:].T.astype(BF16)


def _mla_attn_kernel(q_ref, k_ref, vt_ref, o_ref):
    problems = []
    for i in range(q_ref.shape[1] // LANES):
        sl = slice(i * LANES, (i + 1) * LANES)
        problems.append((k_ref[0, :, sl], q_ref[:, sl], vt_ref[0, i * MLA_V:(i + 1) * MLA_V, :]))
    outs = [ot * (1.0 / l) for ot, l in _attend_t(problems)]
    o_ref[...] = jnp.concatenate(outs, axis=0).T.astype(BF16)


def _l1_out_kernel(o_ref, wo_ref, h_ref, mod_ref, out_ref):
    out_ref[...] = h_ref[...] + mod_ref[0][2:3] * _dot(o_ref[...], wo_ref[...])


def _rope_tables(n_ctx, n_lat, rot_dim, first_lane):
    n_freq = rot_dim // 4
    half = rot_dim // 2
    freqs = ROPE_THETA ** (-jnp.arange(n_freq, dtype=F32) / n_freq)
    rows = n_lat // GRID_W
    row = jnp.repeat(jnp.arange(rows, dtype=F32), GRID_W)
    col = jnp.tile(jnp.arange(GRID_W, dtype=F32), rows)
    ang = jnp.concatenate([row[:, None] * freqs, col[:, None] * freqs], axis=-1)
    ang = jnp.concatenate([jnp.zeros((n_ctx, half), F32), ang], axis=0)
    cos_h, sin_h = jnp.cos(ang), jnp.sin(ang)
    zero = jnp.zeros_like(sin_h)
    n_rep = (LANES - first_lane) // rot_dim if first_lane == 0 else 1
    cos = jnp.concatenate([cos_h, cos_h] * n_rep, axis=-1)
    sa = jnp.concatenate([-sin_h, zero] * n_rep, axis=-1)
    sb = jnp.concatenate([zero, sin_h] * n_rep, axis=-1)
    n_rows = n_ctx + n_lat
    pad_lo = first_lane
    pad_hi = LANES - first_lane - cos.shape[1]
    cos = jnp.concatenate([jnp.ones((n_rows, pad_lo), F32), cos, jnp.ones((n_rows, pad_hi), F32)], axis=-1)
    sa = jnp.pad(sa, ((0, 0), (pad_lo, pad_hi)))
    sb = jnp.pad(sb, ((0, 0), (pad_lo, pad_hi)))
    return cos, sa, sb


def _pad_heads(w, width):
    k = w.shape[0]
    w = w.reshape(k, MLA_HEADS, width)
    return jnp.pad(w, ((0, 0), (0, 0), (0, LANES - width))).reshape(k, MLA_HEADS * LANES)


def kernel(x, c, ctx, c_ctx, mod_w, mod_b, norm_mix_g, norm_ffn_g, ffn_w_gate, ffn_w_up, ffn_w_down, ab_w_in, ab_w_out, pool_w, pool_scale, diff_q_norm_g, diff_k_norm_g, diff_lam_q1, diff_lam_k1, diff_lam_q2, diff_lam_k2, diff_subln_g, mla_w_dq, mla_q_lat_g, mla_w_uq, mla_w_dkv, mla_kv_lat_g, mla_w_ukv, mla_q_norm_g, mla_k_norm_g, mla_w_out):
    n_batch, n_lat, d = x.shape
    n_ctx = ctx.shape[1]
    n_tok = n_ctx + n_lat
    tm = ROW_TILE
    assert n_ctx == tm and n_lat % tm == 0 and n_lat >= 2 * tm and n_batch < MOD_ROWS
    assert mod_w.shape[0] == 2 and ab_w_in.shape[0] == 1 and mla_w_dq.shape[0] == 1
    tpb = n_tok // tm
    lpb = n_lat // tm
    n_rows = n_batch * n_tok
    ffn_hidden = ffn_w_gate.shape[-1]
    ctx_mod = n_batch

    cond = jnp.concatenate([c, c_ctx[None, :], jnp.zeros((MOD_ROWS - n_batch - 1, d), F32)], axis=0)
    mods = _adaln(cond, mod_w, mod_b)
    h = jnp.concatenate([ctx, x], axis=1).reshape(n_rows, d)

    def all_mod(t):
        return (jnp.where(t % tpb == 0, ctx_mod, t // tpb), 0, 0)

    def lat_row(t):
        return (t // lpb) * tpb + 1 + t % lpb

    row_spec = lambda w: pl.BlockSpec((tm, w), lambda t: (t, 0))
    mod_spec = pl.BlockSpec((1, 6, d), all_mod)
    vec = lambda v: v.reshape(1, -1)

    lam_init = 0.8 - 0.6 * math.exp(-0.3 * 0)
    cos, sa, sb = _rope_tables(n_ctx, n_lat, DIFF_HEAD_DIM, 0)
    grp = jnp.arange(DIFF_WIDTH) // DIFF_HEAD_DIM
    block_diag = (grp[:, None] == grp[None, :]).astype(BF16)
    tab_spec = pl.BlockSpec((tm, LANES), lambda t: (t % tpb, 0))
    qkv_shape = jax.ShapeDtypeStruct((n_rows, DIFF_WIDTH), BF16)
    vt_spec = lambda w: pl.BlockSpec((1, w, tm), lambda t: (t // tpb, 0, t % tpb))
    u, q, k, vt = pl.pallas_call(
        _l0_proj_kernel,
        grid=(n_rows // tm,),
        in_specs=[row_spec(d), mod_spec, _const_spec((1, d)), _const_spec(ab_w_in.shape[1:]),
                  _const_spec((1, DIFF_WIDTH)), _const_spec((1, DIFF_WIDTH)),
                  _const_spec((DIFF_WIDTH, DIFF_WIDTH)), tab_spec, tab_spec, tab_spec],
        out_specs=[row_spec(POOL_WIDTH), row_spec(DIFF_WIDTH), row_spec(DIFF_WIDTH), vt_spec(DIFF_WIDTH)],
        out_shape=[jax.ShapeDtypeStruct((n_rows, POOL_WIDTH), BF16), qkv_shape, qkv_shape,
                   jax.ShapeDtypeStruct((n_batch, DIFF_WIDTH, n_tok), BF16)],
        compiler_params=_params("arbitrary"),
        name="l0_proj",
    )(h, mods[0], vec(norm_mix_g[0]), ab_w_in[0].astype(BF16),
      vec(jnp.tile(diff_q_norm_g[0], 2 * DIFF_HEADS)), vec(jnp.tile(diff_k_norm_g[0], 2 * DIFF_HEADS)),
      block_diag, cos, sa, sb)

    dw = DIFF_HEADS_PER_STEP * DIFF_PAIR
    k_spec = pl.BlockSpec((1, n_tok, dw), lambda b, hd, j: (b, 0, hd))
    v_spec = pl.BlockSpec((1, dw, n_tok), lambda b, hd, j: (b, hd, 0))
    qo_spec = pl.BlockSpec((tm, dw), lambda b, hd, j: (b * tpb + j, hd))
    lam_spec = pl.BlockSpec((1, DIFF_HEAD_DIM), lambda b, hd, j: (0, 0))
    o = pl.pallas_call(
        functools.partial(_diff_attn_kernel, n_ctx=n_ctx, lam_init=lam_init),
        grid=(n_batch, DIFF_HEADS // DIFF_HEADS_PER_STEP, tpb),
        in_specs=[qo_spec, k_spec, v_spec, lam_spec, lam_spec, lam_spec, lam_spec,
                  pl.BlockSpec((1, DIFF_PAIR), lambda b, hd, j: (0, 0))],
        out_specs=qo_spec,
        out_shape=qkv_shape,
        compiler_params=_params("arbitrary", "arbitrary", "arbitrary"),
        name="diff_attn",
    )(q, k.reshape(n_batch, n_tok, DIFF_WIDTH), vt,
      vec(diff_lam_q1[0]), vec(diff_lam_k1[0]), vec(diff_lam_q2[0]), vec(diff_lam_k2[0]), vec(diff_subln_g[0]))

    bt_row = lambda w: pl.BlockSpec((tm, w), lambda b, j: (b * tpb + j, 0))
    h = pl.pallas_call(
        functools.partial(_l0_out_kernel, n_ctx=n_ctx, n_tok=n_tok),
        grid=(n_batch, tpb),
        in_specs=[pl.BlockSpec((1, n_tok, POOL_WIDTH), lambda b, j: (b, 0, 0)), bt_row(DIFF_WIDTH),
                  _const_spec(pool_w.shape[1:]), _const_spec((1, POOL_WIDTH)), _const_spec(ab_w_out.shape[1:]),
                  bt_row(d), pl.BlockSpec((1, 6, d), lambda b, j: (jnp.where(j == 0, ctx_mod, b), 0, 0))],
        out_specs=bt_row(d),
        out_shape=jax.ShapeDtypeStruct((n_rows, d), F32),
        compiler_params=_params("arbitrary", "arbitrary"),
        name="l0_out",
    )(u.reshape(n_batch, n_tok, POOL_WIDTH), o, pool_w[0].astype(BF16), vec(pool_scale[0]),
      ab_w_out[0].astype(BF16), h, mods[0])

    ffn_w_specs = [_const_spec((1, d)), _const_spec((d, ffn_hidden)), _const_spec((d, ffn_hidden)),
                   _const_spec((ffn_hidden, d))]
    h = pl.pallas_call(
        _ffn_kernel,
        grid=(n_rows // tm,),
        in_specs=[row_spec(d), mod_spec] + ffn_w_specs,
        out_specs=row_spec(d),
        out_shape=jax.ShapeDtypeStruct((n_rows, d), F32),
        compiler_params=_params("arbitrary"),
        name="ffn0",
    )(h, mods[0], vec(norm_ffn_g[0]), ffn_w_gate[0].astype(BF16), ffn_w_up[0].astype(BF16),
      ffn_w_down[0].astype(BF16))

    cos, sa, sb = _rope_tables(n_ctx, n_lat, MLA_ROPE, MLA_NOPE)
    w_uq = _pad_heads(mla_w_uq[0], MLA_QK).astype(BF16)
    w_dkv = jnp.pad(mla_w_dkv[0], ((0, 0), (0, 0)))
    w_dkv = jnp.concatenate([
        w_dkv[:, :MLA_KV_RANK], jnp.zeros((d, MLA_NOPE), F32), w_dkv[:, MLA_KV_RANK:],
        jnp.zeros((d, LANES - MLA_QK), F32)], axis=-1).astype(BF16)
    w_ukv = mla_w_ukv[0].reshape(MLA_KV_RANK, MLA_HEADS, MLA_NOPE + MLA_V)
    w_ukv = jnp.concatenate([
        _pad_heads(w_ukv[:, :, :MLA_NOPE].reshape(MLA_KV_RANK, -1), MLA_NOPE),
        w_ukv[:, :, MLA_NOPE:].reshape(MLA_KV_RANK, -1)], axis=-1).astype(BF16)
    pad_gain = lambda g: vec(jnp.pad(g, (0, LANES - MLA_QK)))
    hq = MLA_HEADS * LANES
    q, k, vt = pl.pallas_call(
        _l1_proj_kernel,
        grid=(n_rows // tm,),
        in_specs=[row_spec(d), mod_spec, _const_spec((1, d)), _const_spec(mla_w_dq.shape[1:]),
                  _const_spec((1, mla_w_dq.shape[2])), _const_spec(w_uq.shape), _const_spec(w_dkv.shape),
                  _const_spec((1, MLA_KV_RANK)), _const_spec(w_ukv.shape), _const_spec((1, LANES)),
                  _const_spec((1, LANES)), tab_spec, tab_spec, tab_spec],
        out_specs=[row_spec(hq), row_spec(hq), vt_spec(MLA_HEADS * MLA_V)],
        out_shape=[jax.ShapeDtypeStruct((n_rows, hq), BF16), jax.ShapeDtypeStruct((n_rows, hq), BF16),
                   jax.ShapeDtypeStruct((n_batch, MLA_HEADS * MLA_V, n_tok), BF16)],
        compiler_params=_params("arbitrary"),
        name="l1_proj",
    )(h, mods[1], vec(norm_mix_g[1]), mla_w_dq[0].astype(BF16), vec(mla_q_lat_g[0]), w_uq, w_dkv,
      vec(mla_kv_lat_g[0]), w_ukv, pad_gain(mla_q_norm_g[0]), pad_gain(mla_k_norm_g[0]), cos, sa, sb)

    n_lat_rows = n_batch * n_lat
    o = pl.pallas_call(
        _mla_attn_kernel,
        grid=(n_batch, MLA_HEADS // MLA_HEADS_PER_STEP, lpb),
        in_specs=[pl.BlockSpec((tm, MLA_HEADS_PER_STEP * LANES), lambda b, hp, j: (b * tpb + 1 + j, hp)),
                  pl.BlockSpec((1, n_tok, MLA_HEADS_PER_STEP * LANES), lambda b, hp, j: (b, 0, hp)),
                  pl.BlockSpec((1, MLA_HEADS_PER_STEP * MLA_V, n_tok), lambda b, hp, j: (b, hp, 0))],
        out_specs=pl.BlockSpec((tm, MLA_HEADS_PER_STEP * MLA_V), lambda b, hp, j: (b * lpb + j, hp)),
        out_shape=jax.ShapeDtypeStruct((n_lat_rows, MLA_HEADS * MLA_V), BF16),
        compiler_params=_params("arbitrary", "arbitrary", "arbitrary"),
        name="mla_attn",
    )(q, k.reshape(n_batch, n_tok, hq), vt)

    lat_mod_spec = pl.BlockSpec((1, 6, d), lambda t: (t // lpb, 0, 0))
    h = pl.pallas_call(
        _l1_out_kernel,
        grid=(n_lat_rows // tm,),
        in_specs=[row_spec(MLA_HEADS * MLA_V), _const_spec(mla_w_out.shape[1:]),
                  pl.BlockSpec((tm, d), lambda t: (lat_row(t), 0)), lat_mod_spec],
        out_specs=row_spec(d),
        out_shape=jax.ShapeDtypeStruct((n_lat_rows, d), F32),
        compiler_params=_params("arbitrary"),
        name="l1_out",
    )(o, mla_w_out[0].astype(BF16), h, mods[1])

    h = pl.pallas_call(
        _ffn_kernel,
        grid=(n_lat_rows // tm,),
        in_specs=[row_spec(d), lat_mod_spec] + ffn_w_specs,
        out_specs=row_spec(d),
        out_shape=jax.ShapeDtypeStruct((n_lat_rows, d), F32),
        compiler_params=_params("arbitrary"),
        name="ffn1",
    )(h, mods[1], vec(norm_ffn_g[1]), ffn_w_gate[1].astype(BF16), ffn_w_up[1].astype(BF16),
      ffn_w_down[1].astype(BF16))
    return h.reshape(n_batch, n_lat, d)
```

```python
import functools
import math

import jax
import jax.numpy as jnp
from jax import lax
from jax.experimental import pallas as pl
from jax.experimental.pallas import tpu as pltpu

F32 = jnp.float32
BF16 = jnp.bfloat16

EPS = 1e-6
GRID_W = 64
ROPE_THETA = 10000.0
POOL_WINDOWS = (2, 4, 8, 16)
POOL_GROUP = 128
POOL_WIDTH = POOL_GROUP * len(POOL_WINDOWS)
DIFF_HEADS = 4
DIFF_HEAD_DIM = 64
DIFF_PAIR = 2 * DIFF_HEAD_DIM
DIFF_WIDTH = DIFF_HEADS * DIFF_PAIR
MLA_HEADS = 16
MLA_NOPE = 64
MLA_ROPE = 32
MLA_QK = MLA_NOPE + MLA_ROPE
MLA_V = 64
MLA_KV_RANK = 256
LANES = 128
ROW_TILE = 256
MOD_ROWS = 16
ONES_ROWS = 16
LOG2E = math.log2(math.e)
DIFF_HEADS_PER_STEP = 4
MLA_HEADS_PER_STEP = 8
SCORE_LOOKAHEAD = 3
VMEM_LIMIT = 56 * 1024 * 1024

_NT = (((1,), (1,)), ((), ()))


def _dot(a, b):
    return jnp.dot(a, b, preferred_element_type=F32)


def _rms(x, g):
    return x * lax.rsqrt(jnp.mean(x * x, axis=-1, keepdims=True) + EPS) * g


def _params(*sem):
    return pltpu.CompilerParams(dimension_semantics=sem, vmem_limit_bytes=VMEM_LIMIT)


def _const_spec(shape):
    zeros = (0,) * len(shape)
    return pl.BlockSpec(shape, lambda *_: zeros, pipeline_mode=pl.Buffered(1))


def _adaln_kernel(cond_ref, w_ref, b_ref, o_ref):
    c = cond_ref[...]
    a = (c / (1.0 + jnp.exp(-c))).astype(BF16)
    o_ref[0, 0] = _dot(a, w_ref[0].astype(BF16)) + b_ref[0, 0]


def _adaln(cond, mod_w, mod_b):
    depth, d, d6 = mod_w.shape
    n = d6 // d
    out = pl.pallas_call(
        _adaln_kernel,
        grid=(depth, n),
        in_specs=[
            pl.BlockSpec((MOD_ROWS, d), lambda l, j: (0, 0)),
            pl.BlockSpec((1, d, d), lambda l, j: (l, 0, j)),
            pl.BlockSpec((1, 1, 1, d), lambda l, j: (l, j, 0, 0)),
        ],
        out_specs=pl.BlockSpec((1, 1, MOD_ROWS, d), lambda l, j: (l, j, 0, 0)),
        out_shape=jax.ShapeDtypeStruct((depth, n, MOD_ROWS, d), F32),
        compiler_params=_params("arbitrary", "arbitrary"),
        name="adaln",
    )(cond, mod_w, mod_b.reshape(depth, n, 1, d))
    return out.transpose(0, 2, 1, 3)


def _l0_proj_kernel(x_ref, ctx_ref, mod_ref, g_ref, w_ref, qg_ref, kg_ref, bd_ref, cos_ref, sa_ref, sb_ref,
                    u_ref, q_ref, k_ref, vt_ref):
    m = mod_ref[0]
    h = jnp.where(pl.program_id(1) == 0, ctx_ref[0], x_ref[0])
    a = _rms(h, g_ref[...]) * (1.0 + m[1:2]) + m[0:1]
    p = _dot(a.astype(BF16), w_ref[...])
    w = POOL_WIDTH
    cos, sa, sb = cos_ref[...], sa_ref[...], sb_ref[...]

    def norm_rope(z, g):
        ss = _dot((z * z).astype(BF16), bd_ref[...]) * (1.0 / DIFF_HEAD_DIM)
        z = z * lax.rsqrt(ss + EPS) * g
        half = DIFF_HEAD_DIM // 2
        parts = []
        for c in range(z.shape[1] // LANES):
            zc = z[:, c * LANES:(c + 1) * LANES]
            parts.append(zc * cos + pltpu.roll(zc, LANES - half, 1) * sa + pltpu.roll(zc, half, 1) * sb)
        return jnp.concatenate(parts, axis=-1)

    u_ref[...] = p[:, :w].astype(BF16)
    q = norm_rope(p[:, w:w + DIFF_WIDTH], qg_ref[...])
    q_ref[...] = (q * (DIFF_HEAD_DIM ** -0.5 * LOG2E)).astype(BF16)
    k_ref[...] = norm_rope(p[:, w + DIFF_WIDTH:w + 2 * DIFF_WIDTH], kg_ref[...]).astype(BF16)
    vt_ref[0] = p[:, w + 2 * DIFF_WIDTH:].T.astype(BF16)


def _attend_t(problems):
    def scores(p):
        return lax.dot_general(p[0], p[1], _NT, preferred_element_type=F32)

    def finish(st, vt):
        n, dv = st.shape[0], vt.shape[0]
        e = jnp.exp2(st - jnp.max(st, axis=0, keepdims=True)).astype(BF16)
        r = _dot(jnp.concatenate([vt, jnp.ones((ONES_ROWS, n), BF16)], axis=0), e)
        return r[:dv], r[dv:dv + 1]

    outs, sts = [], []
    for i, p in enumerate(problems):
        while len(sts) < min(i + 1 + SCORE_LOOKAHEAD, len(problems)):
            sts.append(scores(problems[len(sts)]))
        outs.append(finish(sts[i], p[2]))
    return outs


def _diff_attn_kernel(q_ref, k_ref, vt_ref, lq1_ref, lk1_ref, lq2_ref, lk2_ref, sg_ref, o_ref, *, n_ctx, lam_init):
    j = pl.program_id(2)
    lam = (jnp.exp(jnp.sum(lq1_ref[...] * lk1_ref[...], axis=-1, keepdims=True))
           - jnp.exp(jnp.sum(lq2_ref[...] * lk2_ref[...], axis=-1, keepdims=True)) + lam_init)
    n_heads = q_ref.shape[1] // DIFF_PAIR
    lane = lax.broadcasted_iota(jnp.int32, (q_ref.shape[0], DIFF_PAIR), 1)

    def attend(n_keys):
        problems = []
        for hd in range(n_heads):
            sl = slice(hd * DIFF_PAIR, (hd + 1) * DIFF_PAIR)
            q = q_ref[:, sl].astype(F32)
            kk, vt = k_ref[0, :n_keys, sl], vt_ref[0, sl, :n_keys]
            problems.append((kk, jnp.where(lane < DIFF_HEAD_DIM, q, 0.0).astype(BF16), vt))
            problems.append((kk, jnp.where(lane >= DIFF_HEAD_DIM, q, 0.0).astype(BF16), vt))
        outs = _attend_t(problems)
        for hd in range(n_heads):
            (o1, l1), (o2, l2) = outs[2 * hd], outs[2 * hd + 1]
            o = (o1 * (1.0 / l1) - o2 * (lam / l2)).T
            o_ref[:, hd * DIFF_PAIR:(hd + 1) * DIFF_PAIR] = (_rms(o, sg_ref[...]) * (1.0 - lam_init)).astype(BF16)

    @pl.when(j == 0)
    def _():
        attend(n_ctx)

    @pl.when(j > 0)
    def _():
        attend(k_ref.shape[1])


def _ffn(h, m, g_ref, wg_ref, wu_ref, wd_ref):
    a = (_rms(h, g_ref[...]) * (1.0 + m[4:5]) + m[3:4]).astype(BF16)
    gate = _dot(a, wg_ref[...])
    up = _dot(a, wu_ref[...])
    hid = (gate / (1.0 + jnp.exp(-gate)) * up).astype(BF16)
    return h + m[5:6] * _dot(hid, wd_ref[...])


def _l0_mix_ffn_kernel(x_ref, ctx_ref, u_ref, o_ref, pw_ref, ps_ref, wo_ref, mod_ref, g_ref, wg_ref, wu_ref, wd_ref,
                       out_ref, *, n_ctx, n_tok):
    tm = out_ref.shape[0]
    win = 2 * tm
    j = pl.program_id(1)
    h = jnp.where(j == 0, ctx_ref[0], x_ref[0])
    t0 = j * tm
    seg_lo = jnp.where(j == 0, 0, n_ctx)
    seg_hi = jnp.where(j == 0, n_ctx, n_tok)
    ws = jnp.where(j == 0, 0, jnp.clip(t0 - tm // 2, n_ctx, n_tok - win))
    uw = u_ref[0, pl.ds(pl.multiple_of(ws, LANES), win), :]
    ut = u_ref[0, pl.ds(pl.multiple_of(t0, tm), tm), :].astype(F32)
    row = t0 + lax.broadcasted_iota(jnp.int32, (tm, 1), 0)
    col = ws + lax.broadcasted_iota(jnp.int32, (tm, win), 1)
    ps = ps_ref[...]
    ys = []
    for g, w in enumerate(POOL_WINDOWS):
        sl = slice(g * POOL_GROUP, (g + 1) * POOL_GROUP)
        lo = jnp.maximum(row - w // 2, seg_lo)
        hi = jnp.minimum(row - w // 2 + w, seg_hi)
        band = jnp.where(col >= lo, jnp.where(col < hi, 1.0, 0.0), 0.0).astype(BF16)
        mean = _dot(band, uw[:, sl]) / (hi - lo).astype(F32)
        y = _dot((mean - ut[:, sl]).astype(BF16), pw_ref[g]) * ps[:, sl]
        ys.append(y.astype(BF16))
    y = jnp.concatenate(ys, axis=-1)
    mixed = _dot(y, wo_ref[:POOL_WIDTH, :]) + _dot(o_ref[...], wo_ref[POOL_WIDTH:, :])
    m = mod_ref[0]
    out_ref[...] = _ffn(h + m[2:3] * mixed, m, g_ref, wg_ref, wu_ref, wd_ref)


def _l1_proj_kernel(h_ref, mod_ref, g_ref, wdq_ref, qlg_ref, wuq_ref, wdkv_ref, kvg_ref, wukv_ref,
                    qg_ref, kg_ref, cos_ref, sa_ref, sb_ref, ones_ref, q_ref, k_ref, vt_ref):
    m = mod_ref[0]
    a = (_rms(h_ref[...], g_ref[...]) * (1.0 + m[1:2]) + m[0:1]).astype(BF16)
    cos, sa, sb = cos_ref[...], sa_ref[...], sb_ref[...]
    half = MLA_ROPE // 2
    pair = 2 * LANES

    def head_sumsq(z):
        parts = [_dot(jnp.square(z[:, p:p + pair]).astype(BF16), ones_ref[...]) for p in range(0, z.shape[1], pair)]
        return jnp.concatenate(parts, axis=-1)

    def rope_tables(g):
        gb = jnp.broadcast_to(g, cos.shape)
        return gb * cos, pltpu.roll(gb, LANES - half, 1) * sa, pltpu.roll(gb, half, 1) * sb

    def rope(z, t):
        return z * t[0] + pltpu.roll(z, LANES - half, 1) * t[1] + pltpu.roll(z, half, 1) * t[2]

    cq = _rms(_dot(a, wdq_ref[...]), qlg_ref[...]).astype(BF16)
    qf = _dot(cq, wuq_ref[...])
    rq = lax.rsqrt(head_sumsq(qf) * (1.0 / MLA_QK) + EPS)
    tq = rope_tables(qg_ref[...] * (MLA_QK ** -0.5 * LOG2E))
    for hd in range(MLA_HEADS):
        sl = slice(hd * LANES, (hd + 1) * LANES)
        q_ref[:, sl] = (rope(qf[:, sl], tq) * rq[:, sl]).astype(BF16)

    ckv = _dot(a, wdkv_ref[...])
    kr = ckv[:, MLA_KV_RANK:]
    ckvn = _rms(ckv[:, :MLA_KV_RANK], kvg_ref[...]).astype(BF16)
    kv = _dot(ckvn, wukv_ref[...])
    kg = kg_ref[...]
    krr = rope(kr, rope_tables(kg))
    kr_ss = _dot(jnp.square(kr).astype(BF16), ones_ref[:LANES, :LANES]) * (1.0 / MLA_QK) + EPS
    kn_ss = head_sumsq(kv[:, :MLA_HEADS * LANES])
    for hd in range(MLA_HEADS):
        sl = slice(hd * LANES, (hd + 1) * LANES)
        rk = lax.rsqrt(kn_ss[:, sl] * (1.0 / MLA_QK) + kr_ss)
        k_ref[:, sl] = (rk * (kv[:, sl] * kg + krr)).astype(BF16)
    vt_ref[0] = kv[:, MLA_HEADS * LANES:].T.astype(BF16)


def _mla_attn_kernel(q_ref, k_ref, vt_ref, o_ref):
    problems = []
    for i in range(q_ref.shape[1] // LANES):
        sl = slice(i * LANES, (i + 1) * LANES)
        problems.append((k_ref[0, :, sl], q_ref[:, sl], vt_ref[0, i * MLA_V:(i + 1) * MLA_V, :]))
    outs = [ot * (1.0 / l) for ot, l in _attend_t(problems)]
    o_ref[...] = jnp.concatenate(outs, axis=0).T.astype(BF16)


def _l1_mix_ffn_kernel(o_ref, wo_ref, h_ref, mod_ref, g_ref, wg_ref, wu_ref, wd_ref, out_ref):
    m = mod_ref[0]
    out_ref[...] = _ffn(h_ref[...] + m[2:3] * _dot(o_ref[...], wo_ref[...]), m, g_ref, wg_ref, wu_ref, wd_ref)


def _rope_tables(n_ctx, n_lat, rot_dim, first_lane):
    n_freq = rot_dim // 4
    half = rot_dim // 2
    freqs = ROPE_THETA ** (-jnp.arange(n_freq, dtype=F32) / n_freq)
    rows = n_lat // GRID_W
    row = jnp.repeat(jnp.arange(rows, dtype=F32), GRID_W)
    col = jnp.tile(jnp.arange(GRID_W, dtype=F32), rows)
    ang = jnp.concatenate([row[:, None] * freqs, col[:, None] * freqs], axis=-1)
    ang = jnp.concatenate([jnp.zeros((n_ctx, half), F32), ang], axis=0)
    cos_h, sin_h = jnp.cos(ang), jnp.sin(ang)
    zero = jnp.zeros_like(sin_h)
    n_rep = (LANES - first_lane) // rot_dim if first_lane == 0 else 1
    cos = jnp.concatenate([cos_h, cos_h] * n_rep, axis=-1)
    sa = jnp.concatenate([-sin_h, zero] * n_rep, axis=-1)
    sb = jnp.concatenate([zero, sin_h] * n_rep, axis=-1)
    n_rows = n_ctx + n_lat
    pad_lo = first_lane
    pad_hi = LANES - first_lane - cos.shape[1]
    cos = jnp.concatenate([jnp.ones((n_rows, pad_lo), F32), cos, jnp.ones((n_rows, pad_hi), F32)], axis=-1)
    sa = jnp.pad(sa, ((0, 0), (pad_lo, pad_hi)))
    sb = jnp.pad(sb, ((0, 0), (pad_lo, pad_hi)))
    return cos, sa, sb


def _pad_heads(w, width):
    k = w.shape[0]
    w = w.reshape(k, MLA_HEADS, width)
    return jnp.pad(w, ((0, 0), (0, 0), (0, LANES - width))).reshape(k, MLA_HEADS * LANES)


def kernel(x, c, ctx, c_ctx, mod_w, mod_b, norm_mix_g, norm_ffn_g, ffn_w_gate, ffn_w_up, ffn_w_down, ab_w_in, ab_w_out, pool_w, pool_scale, diff_q_norm_g, diff_k_norm_g, diff_lam_q1, diff_lam_k1, diff_lam_q2, diff_lam_k2, diff_subln_g, mla_w_dq, mla_q_lat_g, mla_w_uq, mla_w_dkv, mla_kv_lat_g, mla_w_ukv, mla_q_norm_g, mla_k_norm_g, mla_w_out):
    n_batch, n_lat, d = x.shape
    n_ctx = ctx.shape[1]
    n_tok = n_ctx + n_lat
    tm = ROW_TILE
    assert n_ctx == tm and n_lat % tm == 0 and n_lat >= 2 * tm and n_batch < MOD_ROWS
    assert mod_w.shape[0] == 2 and ab_w_in.shape[0] == 1 and mla_w_dq.shape[0] == 1
    tpb = n_tok // tm
    lpb = n_lat // tm
    n_rows = n_batch * n_tok
    ffn_hidden = ffn_w_gate.shape[-1]
    ctx_mod = n_batch

    cond = jnp.concatenate([c, c_ctx[None, :], jnp.zeros((MOD_ROWS - n_batch - 1, d), F32)], axis=0)
    mods = _adaln(cond, mod_w, mod_b)

    def all_mod(t):
        return (jnp.where(t % tpb == 0, ctx_mod, t // tpb), 0, 0)

    def lat_row(t):
        return (t // lpb) * tpb + 1 + t % lpb

    row_spec = lambda w: pl.BlockSpec((tm, w), lambda t: (t, 0))
    mod_spec = pl.BlockSpec((1, 6, d), all_mod)
    vec = lambda v: v.reshape(1, -1)
    bt_row = lambda w: pl.BlockSpec((tm, w), lambda b, j: (b * tpb + j, 0))
    bt_mod = pl.BlockSpec((1, 6, d), lambda b, j: (jnp.where(j == 0, ctx_mod, b), 0, 0))
    x_spec = pl.BlockSpec((1, tm, d), lambda b, j: (b, jnp.maximum(j - 1, 0), 0))
    ctx_spec = pl.BlockSpec((1, tm, d), lambda b, j: (b, 0, 0))
    ffn_w_specs = [_const_spec((1, d)), _const_spec((d, ffn_hidden)), _const_spec((d, ffn_hidden)),
                   _const_spec((ffn_hidden, d))]

    def ffn_weights(layer):
        return (vec(norm_ffn_g[layer]), ffn_w_gate[layer].astype(BF16), ffn_w_up[layer].astype(BF16),
                ffn_w_down[layer].astype(BF16))

    lam_init = 0.8 - 0.6 * math.exp(-0.3 * 0)
    cos, sa, sb = _rope_tables(n_ctx, n_lat, DIFF_HEAD_DIM, 0)
    grp = jnp.arange(DIFF_WIDTH) // DIFF_HEAD_DIM
    block_diag = (grp[:, None] == grp[None, :]).astype(BF16)
    tab_spec = pl.BlockSpec((tm, LANES), lambda t: (t % tpb, 0))
    bt_tab = pl.BlockSpec((tm, LANES), lambda b, j: (j, 0))
    qkv_shape = jax.ShapeDtypeStruct((n_rows, DIFF_WIDTH), BF16)
    vt_spec = lambda w: pl.BlockSpec((1, w, tm), lambda t: (t // tpb, 0, t % tpb))
    u, q, k, vt = pl.pallas_call(
        _l0_proj_kernel,
        grid=(n_batch, tpb),
        in_specs=[x_spec, ctx_spec, bt_mod, _const_spec((1, d)), _const_spec(ab_w_in.shape[1:]),
                  _const_spec((1, DIFF_WIDTH)), _const_spec((1, DIFF_WIDTH)),
                  _const_spec((DIFF_WIDTH, DIFF_WIDTH)), bt_tab, bt_tab, bt_tab],
        out_specs=[bt_row(POOL_WIDTH), bt_row(DIFF_WIDTH), bt_row(DIFF_WIDTH),
                   pl.BlockSpec((1, DIFF_WIDTH, tm), lambda b, j: (b, 0, j))],
        out_shape=[jax.ShapeDtypeStruct((n_rows, POOL_WIDTH), BF16), qkv_shape, qkv_shape,
                   jax.ShapeDtypeStruct((n_batch, DIFF_WIDTH, n_tok), BF16)],
        compiler_params=_params("arbitrary", "arbitrary"),
        name="l0_proj",
    )(x, ctx, mods[0], vec(norm_mix_g[0]), ab_w_in[0].astype(BF16),
      vec(jnp.tile(diff_q_norm_g[0], 2 * DIFF_HEADS)), vec(jnp.tile(diff_k_norm_g[0], 2 * DIFF_HEADS)),
      block_diag, cos, sa, sb)

    dw = DIFF_HEADS_PER_STEP * DIFF_PAIR
    k_spec = pl.BlockSpec((1, n_tok, dw), lambda b, hd, j: (b, 0, hd))
    v_spec = pl.BlockSpec((1, dw, n_tok), lambda b, hd, j: (b, hd, 0))
    qo_spec = pl.BlockSpec((tm, dw), lambda b, hd, j: (b * tpb + j, hd))
    lam_spec = pl.BlockSpec((1, DIFF_HEAD_DIM), lambda b, hd, j: (0, 0))
    o = pl.pallas_call(
        functools.partial(_diff_attn_kernel, n_ctx=n_ctx, lam_init=lam_init),
        grid=(n_batch, DIFF_HEADS // DIFF_HEADS_PER_STEP, tpb),
        in_specs=[qo_spec, k_spec, v_spec, lam_spec, lam_spec, lam_spec, lam_spec,
                  pl.BlockSpec((1, DIFF_PAIR), lambda b, hd, j: (0, 0))],
        out_specs=qo_spec,
        out_shape=qkv_shape,
        compiler_params=_params("arbitrary", "arbitrary", "arbitrary"),
        name="diff_attn",
    )(q, k.reshape(n_batch, n_tok, DIFF_WIDTH), vt,
      vec(diff_lam_q1[0]), vec(diff_lam_k1[0]), vec(diff_lam_q2[0]), vec(diff_lam_k2[0]), vec(diff_subln_g[0]))

    h = pl.pallas_call(
        functools.partial(_l0_mix_ffn_kernel, n_ctx=n_ctx, n_tok=n_tok),
        grid=(n_batch, tpb),
        in_specs=[x_spec, ctx_spec, pl.BlockSpec((1, n_tok, POOL_WIDTH), lambda b, j: (b, 0, 0)), bt_row(DIFF_WIDTH),
                  _const_spec(pool_w.shape[1:]), _const_spec((1, POOL_WIDTH)), _const_spec(ab_w_out.shape[1:]),
                  bt_mod] + ffn_w_specs,
        out_specs=bt_row(d),
        out_shape=jax.ShapeDtypeStruct((n_rows, d), F32),
        compiler_params=_params("arbitrary", "arbitrary"),
        name="l0_mix_ffn",
    )(x, ctx, u.reshape(n_batch, n_tok, POOL_WIDTH), o, pool_w[0].astype(BF16), vec(pool_scale[0]),
      ab_w_out[0].astype(BF16), mods[0], *ffn_weights(0))

    cos, sa, sb = _rope_tables(n_ctx, n_lat, MLA_ROPE, MLA_NOPE)
    w_uq = _pad_heads(mla_w_uq[0], MLA_QK).astype(BF16)
    w_dkv = mla_w_dkv[0]
    w_dkv = jnp.concatenate([
        w_dkv[:, :MLA_KV_RANK], jnp.zeros((d, MLA_NOPE), F32), w_dkv[:, MLA_KV_RANK:],
        jnp.zeros((d, LANES - MLA_QK), F32)], axis=-1).astype(BF16)
    w_ukv = mla_w_ukv[0].reshape(MLA_KV_RANK, MLA_HEADS, MLA_NOPE + MLA_V)
    w_ukv = jnp.concatenate([
        _pad_heads(w_ukv[:, :, :MLA_NOPE].reshape(MLA_KV_RANK, -1), MLA_NOPE),
        w_ukv[:, :, MLA_NOPE:].reshape(MLA_KV_RANK, -1)], axis=-1).astype(BF16)
    pad_gain = lambda g: vec(jnp.pad(g, (0, LANES - MLA_QK)))
    head_of = jnp.arange(2 * LANES) // LANES
    head_ones = (head_of[:, None] == head_of[None, :]).astype(BF16)
    hq = MLA_HEADS * LANES
    q, k, vt = pl.pallas_call(
        _l1_proj_kernel,
        grid=(n_rows // tm,),
        in_specs=[row_spec(d), mod_spec, _const_spec((1, d)), _const_spec(mla_w_dq.shape[1:]),
                  _const_spec((1, mla_w_dq.shape[2])), _const_spec(w_uq.shape), _const_spec(w_dkv.shape),
                  _const_spec((1, MLA_KV_RANK)), _const_spec(w_ukv.shape), _const_spec((1, LANES)),
                  _const_spec((1, LANES)), tab_spec, tab_spec, tab_spec, _const_spec((2 * LANES, 2 * LANES))],
        out_specs=[row_spec(hq), row_spec(hq), vt_spec(MLA_HEADS * MLA_V)],
        out_shape=[jax.ShapeDtypeStruct((n_rows, hq), BF16), jax.ShapeDtypeStruct((n_rows, hq), BF16),
                   jax.ShapeDtypeStruct((n_batch, MLA_HEADS * MLA_V, n_tok), BF16)],
        compiler_params=_params("arbitrary"),
        name="l1_proj",
    )(h, mods[1], vec(norm_mix_g[1]), mla_w_dq[0].astype(BF16), vec(mla_q_lat_g[0]), w_uq, w_dkv,
      vec(mla_kv_lat_g[0]), w_ukv, pad_gain(mla_q_norm_g[0]), pad_gain(mla_k_norm_g[0]), cos, sa, sb,
      head_ones)

    n_lat_rows = n_batch * n_lat
    o = pl.pallas_call(
        _mla_attn_kernel,
        grid=(n_batch, MLA_HEADS // MLA_HEADS_PER_STEP, lpb),
        in_specs=[pl.BlockSpec((tm, MLA_HEADS_PER_STEP * LANES), lambda b, hp, j: (b * tpb + 1 + j, hp)),
                  pl.BlockSpec((1, n_tok, MLA_HEADS_PER_STEP * LANES), lambda b, hp, j: (b, 0, hp)),
                  pl.BlockSpec((1, MLA_HEADS_PER_STEP * MLA_V, n_tok), lambda b, hp, j: (b, hp, 0))],
        out_specs=pl.BlockSpec((tm, MLA_HEADS_PER_STEP * MLA_V), lambda b, hp, j: (b * lpb + j, hp)),
        out_shape=jax.ShapeDtypeStruct((n_lat_rows, MLA_HEADS * MLA_V), BF16),
        compiler_params=_params("arbitrary", "arbitrary", "arbitrary"),
        name="mla_attn",
    )(q, k.reshape(n_batch, n_tok, hq), vt)

    lat_mod_spec = pl.BlockSpec((1, 6, d), lambda t: (t // lpb, 0, 0))
    h = pl.pallas_call(
        _l1_mix_ffn_kernel,
        grid=(n_lat_rows // tm,),
        in_specs=[row_spec(MLA_HEADS * MLA_V), _const_spec(mla_w_out.shape[1:]),
                  pl.BlockSpec((tm, d), lambda t: (lat_row(t), 0)), lat_mod_spec] + ffn_w_specs,
        out_specs=row_spec(d),
        out_shape=jax.ShapeDtypeStruct((n_lat_rows, d), F32),
        compiler_params=_params("arbitrary"),
        name="l1_mix_ffn",
    )(o, mla_w_out[0].astype(BF16), h, mods[1], *ffn_weights(1))
    return h.reshape(n_batch, n_lat, d)
```

```python
import functools
import math

import jax
import jax.numpy as jnp
from jax import lax
from jax.experimental import pallas as pl
from jax.experimental.pallas import tpu as pltpu

F32 = jnp.float32
BF16 = jnp.bfloat16

EPS = 1e-6
GRID_W = 64
ROPE_THETA = 10000.0
POOL_WINDOWS = (2, 4, 8, 16)
POOL_GROUP = 128
POOL_WIDTH = POOL_GROUP * len(POOL_WINDOWS)
DIFF_HEADS = 4
DIFF_HEAD_DIM = 64
DIFF_PAIR = 2 * DIFF_HEAD_DIM
DIFF_WIDTH = DIFF_HEADS * DIFF_PAIR
MLA_HEADS = 16
MLA_NOPE = 64
MLA_ROPE = 32
MLA_QK = MLA_NOPE + MLA_ROPE
MLA_V = 64
MLA_KV_RANK = 256
LANES = 128
ROW_TILE = 256
COL_BLOCK = 256
MOD_ROWS = 16
ONES_ROWS = 16
LOG2E = math.log2(math.e)
DIFF_HEADS_PER_STEP = 4
MLA_HEADS_PER_STEP = 16
SCORE_LOOKAHEAD = 4
VMEM_LIMIT = 56 * 1024 * 1024

_NT = (((1,), (1,)), ((), ()))


def _dot(a, b):
    return jnp.dot(a, b, preferred_element_type=F32)


def _rms(x, g):
    return x * lax.rsqrt(jnp.mean(x * x, axis=-1, keepdims=True) + EPS) * g


def _rms_mod(x, g, shift, scale):
    return x * lax.rsqrt(jnp.mean(x * x, axis=-1, keepdims=True) + EPS) * (g * (1.0 + scale)) + shift


def _params(*sem):
    return pltpu.CompilerParams(dimension_semantics=sem, vmem_limit_bytes=VMEM_LIMIT)


def _const_spec(shape):
    zeros = (0,) * len(shape)
    return pl.BlockSpec(shape, lambda *_: zeros, pipeline_mode=pl.Buffered(1))


def _adaln_kernel(cond_ref, w_ref, b_ref, o_ref):
    c = cond_ref[...]
    a = (c / (1.0 + jnp.exp(-c))).astype(BF16)
    o_ref[0, 0] = _dot(a, w_ref[0].astype(BF16)) + b_ref[0, 0]


def _adaln(cond, mod_w, mod_b):
    depth, d, d6 = mod_w.shape
    n = d6 // d
    out = pl.pallas_call(
        _adaln_kernel,
        grid=(depth, n),
        in_specs=[
            pl.BlockSpec((MOD_ROWS, d), lambda l, j: (0, 0)),
            pl.BlockSpec((1, d, d), lambda l, j: (l, 0, j)),
            pl.BlockSpec((1, 1, 1, d), lambda l, j: (l, j, 0, 0)),
        ],
        out_specs=pl.BlockSpec((1, 1, MOD_ROWS, d), lambda l, j: (l, j, 0, 0)),
        out_shape=jax.ShapeDtypeStruct((depth, n, MOD_ROWS, d), F32),
        compiler_params=_params("arbitrary", "arbitrary"),
        name="adaln",
    )(cond, mod_w, mod_b.reshape(depth, n, 1, d))
    return out.transpose(0, 2, 1, 3)


def _project_columns(items):
    project = lambda it: _dot(it[2], it[3][:, it[4] * COL_BLOCK:(it[4] + 1) * COL_BLOCK])
    z_next = project(items[0])
    for i, it in enumerate(items):
        z, z_next = z_next, (project(items[i + 1]) if i + 1 < len(items) else None)
        it[0](it[1], z)


def _rope_fns(cos, sa, sb, half):
    def tables(g):
        gb = jnp.broadcast_to(g, cos.shape)
        return gb * cos, pltpu.roll(gb, LANES - half, 1) * sa, pltpu.roll(gb, half, 1) * sb

    def rope(z, t):
        return z * t[0] + pltpu.roll(z, LANES - half, 1) * t[1] + pltpu.roll(z, half, 1) * t[2]

    return tables, rope


def _l0_proj_kernel(x_ref, ctx_ref, mod_ref, g_ref, w_ref, qg_ref, kg_ref, bd_ref, cos_ref, sa_ref, sb_ref,
                    u_ref, q_ref, k_ref, vt_ref):
    m = mod_ref[0]
    h = jnp.where(pl.program_id(1) == 0, ctx_ref[0], x_ref[0])
    a = _rms_mod(h, g_ref[...], m[0:1], m[1:2]).astype(BF16)
    tables, rope = _rope_fns(cos_ref[...], sa_ref[...], sb_ref[...], DIFF_HEAD_DIM // 2)
    root_n, n_eps = DIFF_HEAD_DIM ** 0.5, DIFF_HEAD_DIM * EPS
    tq = tables(qg_ref[...] * (root_n * DIFF_HEAD_DIM ** -0.5 * LOG2E))
    tk = tables(kg_ref[...] * root_n)

    def plain(ref):
        def finish(p, z):
            ref[:, p * COL_BLOCK:(p + 1) * COL_BLOCK] = z.astype(BF16)
        return finish

    def norm_rope(ref, t):
        def finish(p, z):
            r = lax.rsqrt(_dot(jnp.square(z).astype(BF16), bd_ref[...]) + n_eps)
            for i in range(COL_BLOCK // LANES):
                sl = slice(i * LANES, (i + 1) * LANES)
                ref[:, p * COL_BLOCK + i * LANES:p * COL_BLOCK + (i + 1) * LANES] = (rope(z[:, sl], t) * r[:, sl]).astype(BF16)
        return finish

    def transposed(p, z):
        vt_ref[0, p * COL_BLOCK:(p + 1) * COL_BLOCK, :] = z.T.astype(BF16)

    items, col = [], 0
    for finish, width in ((plain(u_ref), POOL_WIDTH), (norm_rope(q_ref, tq), DIFF_WIDTH),
                          (norm_rope(k_ref, tk), DIFF_WIDTH), (transposed, DIFF_WIDTH)):
        items += [(finish, p, a, w_ref, col + p) for p in range(width // COL_BLOCK)]
        col += width // COL_BLOCK
    _project_columns(items)


def _attend_t(problems):
    def scores(p):
        return lax.dot_general(p[0], p[1], _NT, preferred_element_type=F32)

    def finish(st, vt):
        n, dv = st.shape[0], vt.shape[0]
        e = jnp.exp2(st - jnp.max(st, axis=0, keepdims=True)).astype(BF16)
        r = _dot(jnp.concatenate([vt, jnp.ones((ONES_ROWS, n), BF16)], axis=0), e)
        return r[:dv], r[dv:dv + 1]

    outs, sts = [], []
    for i, p in enumerate(problems):
        while len(sts) < min(i + 1 + SCORE_LOOKAHEAD, len(problems)):
            sts.append(scores(problems[len(sts)]))
        outs.append(finish(sts[i], p[2]))
    return outs


def _diff_attn_kernel(q_ref, k_ref, vt_ref, lq1_ref, lk1_ref, lq2_ref, lk2_ref, sg_ref, o_ref, *, n_ctx, lam_init):
    j = pl.program_id(2)
    lam = (jnp.exp(jnp.sum(lq1_ref[...] * lk1_ref[...], axis=-1, keepdims=True))
           - jnp.exp(jnp.sum(lq2_ref[...] * lk2_ref[...], axis=-1, keepdims=True)) + lam_init)
    n_heads = q_ref.shape[1] // DIFF_PAIR
    lane = lax.broadcasted_iota(jnp.int32, (q_ref.shape[0], DIFF_PAIR), 1)

    def attend(n_keys):
        problems = []
        for hd in range(n_heads):
            sl = slice(hd * DIFF_PAIR, (hd + 1) * DIFF_PAIR)
            q = q_ref[:, sl].astype(F32)
            kk, vt = k_ref[0, :n_keys, sl], vt_ref[0, sl, :n_keys]
            problems.append((kk, jnp.where(lane < DIFF_HEAD_DIM, q, 0.0).astype(BF16), vt))
            problems.append((kk, jnp.where(lane >= DIFF_HEAD_DIM, q, 0.0).astype(BF16), vt))
        outs = _attend_t(problems)
        for hd in range(n_heads):
            (o1, l1), (o2, l2) = outs[2 * hd], outs[2 * hd + 1]
            o = (o1 * (1.0 / l1) - o2 * (lam / l2)).T
            o_ref[:, hd * DIFF_PAIR:(hd + 1) * DIFF_PAIR] = (_rms(o, sg_ref[...]) * (1.0 - lam_init)).astype(BF16)

    @pl.when(j == 0)
    def _():
        attend(n_ctx)

    @pl.when(j > 0)
    def _():
        attend(k_ref.shape[1])


def _ffn(h, m, g_ref, wg_ref, wu_ref, wd_ref):
    a = _rms_mod(h, g_ref[...], m[3:4], m[4:5]).astype(BF16)
    gate = _dot(a, wg_ref[...])
    up = _dot(a, wu_ref[...])
    hid = (gate / (1.0 + jnp.exp(-gate)) * up).astype(BF16)
    return h + m[5:6] * _dot(hid, wd_ref[...])


def _l0_mix_ffn_kernel(x_ref, ctx_ref, u_ref, o_ref, pw_ref, ps_ref, wo_ref, mod_ref, g_ref, wg_ref, wu_ref, wd_ref,
                       out_ref, *, n_ctx, n_tok):
    tm = out_ref.shape[0]
    win = 2 * tm
    j = pl.program_id(1)
    h = jnp.where(j == 0, ctx_ref[0], x_ref[0])
    t0 = j * tm
    seg_lo = jnp.where(j == 0, 0, n_ctx)
    seg_hi = jnp.where(j == 0, n_ctx, n_tok)
    ws = jnp.where(j == 0, 0, jnp.clip(t0 - tm // 2, n_ctx, n_tok - win))
    uw = u_ref[0, pl.ds(pl.multiple_of(ws, LANES), win), :]
    ut = u_ref[0, pl.ds(pl.multiple_of(t0, tm), tm), :].astype(F32)
    row = t0 + lax.broadcasted_iota(jnp.int32, (tm, 1), 0)
    col = ws + lax.broadcasted_iota(jnp.int32, (tm, win), 1)
    ps = ps_ref[...]
    ys = []
    for g, w in enumerate(POOL_WINDOWS):
        sl = slice(g * POOL_GROUP, (g + 1) * POOL_GROUP)
        lo = jnp.maximum(row - w // 2, seg_lo)
        hi = jnp.minimum(row - w // 2 + w, seg_hi)
        band = jnp.where(col >= lo, jnp.where(col < hi, 1.0, 0.0), 0.0).astype(BF16)
        mean = _dot(band, uw[:, sl]) / (hi - lo).astype(F32)
        y = _dot((mean - ut[:, sl]).astype(BF16), pw_ref[g]) * ps[:, sl]
        ys.append(y.astype(BF16))
    y = jnp.concatenate(ys, axis=-1)
    mixed = _dot(y, wo_ref[:POOL_WIDTH, :]) + _dot(o_ref[...], wo_ref[POOL_WIDTH:, :])
    m = mod_ref[0]
    out_ref[...] = _ffn(h + m[2:3] * mixed, m, g_ref, wg_ref, wu_ref, wd_ref)


def _l1_proj_kernel(h_ref, mod_ref, g_ref, wdq_ref, qlg_ref, wuq_ref, wdkv_ref, kvg_ref, wukv_ref,
                    qg_ref, kg_ref, cos_ref, sa_ref, sb_ref, ones_ref, q_ref, k_ref, vt_ref):
    m = mod_ref[0]
    a = _rms_mod(h_ref[...], g_ref[...], m[0:1], m[1:2]).astype(BF16)
    tables, rope = _rope_fns(cos_ref[...], sa_ref[...], sb_ref[...], MLA_ROPE // 2)

    def head_sumsq(z):
        return _dot(jnp.square(z).astype(BF16), ones_ref[...])

    cq = _rms(_dot(a, wdq_ref[...]), qlg_ref[...]).astype(BF16)
    ckv = _dot(a, wdkv_ref[...])
    kr = ckv[:, MLA_KV_RANK:]
    ckvn = _rms(ckv[:, :MLA_KV_RANK], kvg_ref[...]).astype(BF16)
    root_n, n_eps = MLA_QK ** 0.5, MLA_QK * EPS
    tq = tables(qg_ref[...] * (root_n * MLA_QK ** -0.5 * LOG2E))
    kg = kg_ref[...] * root_n
    krr = rope(kr, tables(kg))
    kr_ss = _dot(jnp.square(kr).astype(BF16), ones_ref[:LANES, :LANES]) + n_eps
    heads = [(i, slice(i * LANES, (i + 1) * LANES)) for i in range(COL_BLOCK // LANES)]

    def q_pair(p, z):
        r = lax.rsqrt(head_sumsq(z) + n_eps)
        for i, sl in heads:
            q_ref[:, p * COL_BLOCK + i * LANES:p * COL_BLOCK + (i + 1) * LANES] = (rope(z[:, sl], tq) * r[:, sl]).astype(BF16)

    def k_pair(p, z):
        ss = head_sumsq(z)
        for i, sl in heads:
            rk = lax.rsqrt(ss[:, sl] + kr_ss)
            k_ref[:, p * COL_BLOCK + i * LANES:p * COL_BLOCK + (i + 1) * LANES] = (rk * (z[:, sl] * kg + krr)).astype(BF16)

    def v_pair(p, z):
        vt_ref[0, p * COL_BLOCK:(p + 1) * COL_BLOCK, :] = z.T.astype(BF16)

    n_kp = MLA_HEADS * LANES // COL_BLOCK
    items = [(q_pair, p, cq, wuq_ref, p) for p in range(n_kp)]
    items += [(k_pair, p, ckvn, wukv_ref, p) for p in range(n_kp)]
    items += [(v_pair, p, ckvn, wukv_ref, n_kp + p) for p in range(MLA_HEADS * MLA_V // COL_BLOCK)]
    _project_columns(items)


def _mla_attn_kernel(q_ref, k_ref, vt_ref, o_ref):
    problems = []
    for i in range(q_ref.shape[1] // LANES):
        sl = slice(i * LANES, (i + 1) * LANES)
        problems.append((k_ref[0, :, sl], q_ref[:, sl], vt_ref[0, i * MLA_V:(i + 1) * MLA_V, :]))
    outs = [ot * (1.0 / l) for ot, l in _attend_t(problems)]
    o_ref[...] = jnp.concatenate(outs, axis=0).T.astype(BF16)


def _l1_mix_ffn_kernel(o_ref, wo_ref, h_ref, mod_ref, g_ref, wg_ref, wu_ref, wd_ref, out_ref):
    m = mod_ref[0]
    out_ref[...] = _ffn(h_ref[...] + m[2:3] * _dot(o_ref[...], wo_ref[...]), m, g_ref, wg_ref, wu_ref, wd_ref)


def _rope_tables(n_ctx, n_lat, rot_dim, first_lane):
    n_freq = rot_dim // 4
    half = rot_dim // 2
    freqs = ROPE_THETA ** (-jnp.arange(n_freq, dtype=F32) / n_freq)
    rows = n_lat // GRID_W
    row = jnp.repeat(jnp.arange(rows, dtype=F32), GRID_W)
    col = jnp.tile(jnp.arange(GRID_W, dtype=F32), rows)
    ang = jnp.concatenate([row[:, None] * freqs, col[:, None] * freqs], axis=-1)
    ang = jnp.concatenate([jnp.zeros((n_ctx, half), F32), ang], axis=0)
    cos_h, sin_h = jnp.cos(ang), jnp.sin(ang)
    zero = jnp.zeros_like(sin_h)
    n_rep = (LANES - first_lane) // rot_dim if first_lane == 0 else 1
    cos = jnp.concatenate([cos_h, cos_h] * n_rep, axis=-1)
    sa = jnp.concatenate([-sin_h, zero] * n_rep, axis=-1)
    sb = jnp.concatenate([zero, sin_h] * n_rep, axis=-1)
    n_rows = n_ctx + n_lat
    pad_lo = first_lane
    pad_hi = LANES - first_lane - cos.shape[1]
    cos = jnp.concatenate([jnp.ones((n_rows, pad_lo), F32), cos, jnp.ones((n_rows, pad_hi), F32)], axis=-1)
    sa = jnp.pad(sa, ((0, 0), (pad_lo, pad_hi)))
    sb = jnp.pad(sb, ((0, 0), (pad_lo, pad_hi)))
    return cos, sa, sb


def _pad_heads(w, width):
    k = w.shape[0]
    w = w.reshape(k, MLA_HEADS, width)
    return jnp.pad(w, ((0, 0), (0, 0), (0, LANES - width))).reshape(k, MLA_HEADS * LANES)


def kernel(x, c, ctx, c_ctx, mod_w, mod_b, norm_mix_g, norm_ffn_g, ffn_w_gate, ffn_w_up, ffn_w_down, ab_w_in, ab_w_out, pool_w, pool_scale, diff_q_norm_g, diff_k_norm_g, diff_lam_q1, diff_lam_k1, diff_lam_q2, diff_lam_k2, diff_subln_g, mla_w_dq, mla_q_lat_g, mla_w_uq, mla_w_dkv, mla_kv_lat_g, mla_w_ukv, mla_q_norm_g, mla_k_norm_g, mla_w_out):
    n_batch, n_lat, d = x.shape
    n_ctx = ctx.shape[1]
    n_tok = n_ctx + n_lat
    tm = ROW_TILE
    assert n_ctx == tm and n_lat % tm == 0 and n_lat >= 2 * tm and n_batch < MOD_ROWS
    assert mod_w.shape[0] == 2 and ab_w_in.shape[0] == 1 and mla_w_dq.shape[0] == 1
    tpb = n_tok // tm
    lpb = n_lat // tm
    n_rows = n_batch * n_tok
    ffn_hidden = ffn_w_gate.shape[-1]
    ctx_mod = n_batch

    cond = jnp.concatenate([c, c_ctx[None, :], jnp.zeros((MOD_ROWS - n_batch - 1, d), F32)], axis=0)
    mods = _adaln(cond, mod_w, mod_b)

    def all_mod(t):
        return (jnp.where(t % tpb == 0, ctx_mod, t // tpb), 0, 0)

    def lat_row(t):
        return (t // lpb) * tpb + 1 + t % lpb

    row_spec = lambda w: pl.BlockSpec((tm, w), lambda t: (t, 0))
    mod_spec = pl.BlockSpec((1, 6, d), all_mod)
    vec = lambda v: v.reshape(1, -1)
    bt_row = lambda w: pl.BlockSpec((tm, w), lambda b, j: (b * tpb + j, 0))
    bt_mod = pl.BlockSpec((1, 6, d), lambda b, j: (jnp.where(j == 0, ctx_mod, b), 0, 0))
    x_spec = pl.BlockSpec((1, tm, d), lambda b, j: (b, jnp.maximum(j - 1, 0), 0))
    ctx_spec = pl.BlockSpec((1, tm, d), lambda b, j: (b, 0, 0))
    ffn_w_specs = [_const_spec((1, d)), _const_spec((d, ffn_hidden)), _const_spec((d, ffn_hidden)),
                   _const_spec((ffn_hidden, d))]

    def ffn_weights(layer):
        return (vec(norm_ffn_g[layer]), ffn_w_gate[layer].astype(BF16), ffn_w_up[layer].astype(BF16),
                ffn_w_down[layer].astype(BF16))

    lam_init = 0.8 - 0.6 * math.exp(-0.3 * 0)
    cos, sa, sb = _rope_tables(n_ctx, n_lat, DIFF_HEAD_DIM, 0)
    grp = jnp.arange(COL_BLOCK) // DIFF_HEAD_DIM
    block_diag = (grp[:, None] == grp[None, :]).astype(BF16)
    tab_spec = pl.BlockSpec((tm, LANES), lambda t: (t % tpb, 0))
    bt_tab = pl.BlockSpec((tm, LANES), lambda b, j: (j, 0))
    qkv_shape = jax.ShapeDtypeStruct((n_rows, DIFF_WIDTH), BF16)
    vt_spec = lambda w: pl.BlockSpec((1, w, tm), lambda t: (t // tpb, 0, t % tpb))
    u, q, k, vt = pl.pallas_call(
        _l0_proj_kernel,
        grid=(n_batch, tpb),
        in_specs=[x_spec, ctx_spec, bt_mod, _const_spec((1, d)), _const_spec(ab_w_in.shape[1:]),
                  _const_spec((1, LANES)), _const_spec((1, LANES)),
                  _const_spec((COL_BLOCK, COL_BLOCK)), bt_tab, bt_tab, bt_tab],
        out_specs=[bt_row(POOL_WIDTH), bt_row(DIFF_WIDTH), bt_row(DIFF_WIDTH),
                   pl.BlockSpec((1, DIFF_WIDTH, tm), lambda b, j: (b, 0, j))],
        out_shape=[jax.ShapeDtypeStruct((n_rows, POOL_WIDTH), BF16), qkv_shape, qkv_shape,
                   jax.ShapeDtypeStruct((n_batch, DIFF_WIDTH, n_tok), BF16)],
        compiler_params=_params("arbitrary", "arbitrary"),
        name="l0_proj",
    )(x, ctx, mods[0], vec(norm_mix_g[0]), ab_w_in[0].astype(BF16),
      vec(jnp.tile(diff_q_norm_g[0], LANES // DIFF_HEAD_DIM)), vec(jnp.tile(diff_k_norm_g[0], LANES // DIFF_HEAD_DIM)),
      block_diag, cos, sa, sb)

    dw = DIFF_HEADS_PER_STEP * DIFF_PAIR
    k_spec = pl.BlockSpec((1, n_tok, dw), lambda b, hd, j: (b, 0, hd))
    v_spec = pl.BlockSpec((1, dw, n_tok), lambda b, hd, j: (b, hd, 0))
    qo_spec = pl.BlockSpec((tm, dw), lambda b, hd, j: (b * tpb + j, hd))
    lam_spec = pl.BlockSpec((1, DIFF_HEAD_DIM), lambda b, hd, j: (0, 0))
    o = pl.pallas_call(
        functools.partial(_diff_attn_kernel, n_ctx=n_ctx, lam_init=lam_init),
        grid=(n_batch, DIFF_HEADS // DIFF_HEADS_PER_STEP, tpb),
        in_specs=[qo_spec, k_spec, v_spec, lam_spec, lam_spec, lam_spec, lam_spec,
                  pl.BlockSpec((1, DIFF_PAIR), lambda b, hd, j: (0, 0))],
        out_specs=qo_spec,
        out_shape=qkv_shape,
        compiler_params=_params("arbitrary", "arbitrary", "arbitrary"),
        name="diff_attn",
    )(q, k.reshape(n_batch, n_tok, DIFF_WIDTH), vt,
      vec(diff_lam_q1[0]), vec(diff_lam_k1[0]), vec(diff_lam_q2[0]), vec(diff_lam_k2[0]), vec(diff_subln_g[0]))

    h = pl.pallas_call(
        functools.partial(_l0_mix_ffn_kernel, n_ctx=n_ctx, n_tok=n_tok),
        grid=(n_batch, tpb),
        in_specs=[x_spec, ctx_spec, pl.BlockSpec((1, n_tok, POOL_WIDTH), lambda b, j: (b, 0, 0)), bt_row(DIFF_WIDTH),
                  _const_spec(pool_w.shape[1:]), _const_spec((1, POOL_WIDTH)), _const_spec(ab_w_out.shape[1:]),
                  bt_mod] + ffn_w_specs,
        out_specs=bt_row(d),
        out_shape=jax.ShapeDtypeStruct((n_rows, d), F32),
        compiler_params=_params("arbitrary", "arbitrary"),
        name="l0_mix_ffn",
    )(x, ctx, u.reshape(n_batch, n_tok, POOL_WIDTH), o, pool_w[0].astype(BF16), vec(pool_scale[0]),
      ab_w_out[0].astype(BF16), mods[0], *ffn_weights(0))

    cos, sa, sb = _rope_tables(n_ctx, n_lat, MLA_ROPE, MLA_NOPE)
    w_uq = _pad_heads(mla_w_uq[0], MLA_QK).astype(BF16)
    w_dkv = mla_w_dkv[0]
    w_dkv = jnp.concatenate([
        w_dkv[:, :MLA_KV_RANK], jnp.zeros((d, MLA_NOPE), F32), w_dkv[:, MLA_KV_RANK:],
        jnp.zeros((d, LANES - MLA_QK), F32)], axis=-1).astype(BF16)
    w_ukv = mla_w_ukv[0].reshape(MLA_KV_RANK, MLA_HEADS, MLA_NOPE + MLA_V)
    w_ukv = jnp.concatenate([
        _pad_heads(w_ukv[:, :, :MLA_NOPE].reshape(MLA_KV_RANK, -1), MLA_NOPE),
        w_ukv[:, :, MLA_NOPE:].reshape(MLA_KV_RANK, -1)], axis=-1).astype(BF16)
    pad_gain = lambda g: vec(jnp.pad(g, (0, LANES - MLA_QK)))
    head_of = jnp.arange(2 * LANES) // LANES
    head_ones = (head_of[:, None] == head_of[None, :]).astype(BF16)
    hq = MLA_HEADS * LANES
    q, k, vt = pl.pallas_call(
        _l1_proj_kernel,
        grid=(n_rows // tm,),
        in_specs=[row_spec(d), mod_spec, _const_spec((1, d)), _const_spec(mla_w_dq.shape[1:]),
                  _const_spec((1, mla_w_dq.shape[2])), _const_spec(w_uq.shape), _const_spec(w_dkv.shape),
                  _const_spec((1, MLA_KV_RANK)), _const_spec(w_ukv.shape), _const_spec((1, LANES)),
                  _const_spec((1, LANES)), tab_spec, tab_spec, tab_spec, _const_spec((2 * LANES, 2 * LANES))],
        out_specs=[row_spec(hq), row_spec(hq), vt_spec(MLA_HEADS * MLA_V)],
        out_shape=[jax.ShapeDtypeStruct((n_rows, hq), BF16), jax.ShapeDtypeStruct((n_rows, hq), BF16),
                   jax.ShapeDtypeStruct((n_batch, MLA_HEADS * MLA_V, n_tok), BF16)],
        compiler_params=_params("arbitrary"),
        name="l1_proj",
    )(h, mods[1], vec(norm_mix_g[1]), mla_w_dq[0].astype(BF16), vec(mla_q_lat_g[0]), w_uq, w_dkv,
      vec(mla_kv_lat_g[0]), w_ukv, pad_gain(mla_q_norm_g[0]), pad_gain(mla_k_norm_g[0]), cos, sa, sb,
      head_ones)

    n_lat_rows = n_batch * n_lat
    o = pl.pallas_call(
        _mla_attn_kernel,
        grid=(n_batch, MLA_HEADS // MLA_HEADS_PER_STEP, lpb),
        in_specs=[pl.BlockSpec((tm, MLA_HEADS_PER_STEP * LANES), lambda b, hp, j: (b * tpb + 1 + j, hp)),
                  pl.BlockSpec((1, n_tok, MLA_HEADS_PER_STEP * LANES), lambda b, hp, j: (b, 0, hp)),
                  pl.BlockSpec((1, MLA_HEADS_PER_STEP * MLA_V, n_tok), lambda b, hp, j: (b, hp, 0))],
        out_specs=pl.BlockSpec((tm, MLA_HEADS_PER_STEP * MLA_V), lambda b, hp, j: (b * lpb + j, hp)),
        out_shape=jax.ShapeDtypeStruct((n_lat_rows, MLA_HEADS * MLA_V), BF16),
        compiler_params=_params("arbitrary", "arbitrary", "arbitrary"),
        name="mla_attn",
    )(q, k.reshape(n_batch, n_tok, hq), vt)

    lat_mod_spec = pl.BlockSpec((1, 6, d), lambda t: (t // lpb, 0, 0))
    h = pl.pallas_call(
        _l1_mix_ffn_kernel,
        grid=(n_lat_rows // tm,),
        in_specs=[row_spec(MLA_HEADS * MLA_V), _const_spec(mla_w_out.shape[1:]),
                  pl.BlockSpec((tm, d), lambda t: (lat_row(t), 0)), lat_mod_spec] + ffn_w_specs,
        out_specs=row_spec(d),
        out_shape=jax.ShapeDtypeStruct((n_lat_rows, d), F32),
        compiler_params=_params("arbitrary"),
        name="l1_mix_ffn",
    )(o, mla_w_out[0].astype(BF16), h, mods[1], *ffn_weights(1))
    return h.reshape(n_batch, n_lat, d)
```

```python
import functools
import math

import jax
import jax.numpy as jnp
from jax import lax
from jax.experimental import pallas as pl
from jax.experimental.pallas import tpu as pltpu

F32 = jnp.float32
BF16 = jnp.bfloat16

EPS = 1e-6
GRID_W = 64
ROPE_THETA = 10000.0
POOL_WINDOWS = (2, 4, 8, 16)
POOL_GROUP = 128
POOL_WIDTH = POOL_GROUP * len(POOL_WINDOWS)
DIFF_HEADS = 4
DIFF_HEAD_DIM = 64
DIFF_PAIR = 2 * DIFF_HEAD_DIM
DIFF_WIDTH = DIFF_HEADS * DIFF_PAIR
MLA_HEADS = 16
MLA_NOPE = 64
MLA_ROPE = 32
MLA_QK = MLA_NOPE + MLA_ROPE
MLA_V = 64
MLA_KV_RANK = 256
LANES = 128
ROW_TILE = 256
COL_BLOCK = 256
MOD_ROWS = 16
ONES_ROWS = 16
LOG2E = math.log2(math.e)
DIFF_HEADS_PER_STEP = 4
MLA_HEADS_PER_STEP = 16
KEY_CHUNK = 256
DIFF_LOOKAHEAD = 3
MLA_LOOKAHEAD = 1
VMEM_LIMIT = 56 * 1024 * 1024

_NT = (((1,), (1,)), ((), ()))


def _dot(a, b):
    return jnp.dot(a, b, preferred_element_type=F32)


def _rms(x, g):
    return x * lax.rsqrt(jnp.mean(x * x, axis=-1, keepdims=True) + EPS) * g


def _rms_mod(x, g, shift, scale):
    return x * lax.rsqrt(jnp.mean(x * x, axis=-1, keepdims=True) + EPS) * (g * (1.0 + scale)) + shift


def _params(*sem):
    return pltpu.CompilerParams(dimension_semantics=sem, vmem_limit_bytes=VMEM_LIMIT)


def _const_spec(shape):
    zeros = (0,) * len(shape)
    return pl.BlockSpec(shape, lambda *_: zeros, pipeline_mode=pl.Buffered(1))


def _adaln_kernel(cond_ref, w_ref, b_ref, o_ref):
    c = cond_ref[...]
    a = (c / (1.0 + jnp.exp(-c))).astype(BF16)
    o_ref[0, 0] = _dot(a, w_ref[0].astype(BF16)) + b_ref[0, 0]


def _adaln(cond, mod_w, mod_b):
    depth, d, d6 = mod_w.shape
    n = d6 // d
    out = pl.pallas_call(
        _adaln_kernel,
        grid=(depth, n),
        in_specs=[
            pl.BlockSpec((MOD_ROWS, d), lambda l, j: (0, 0)),
            pl.BlockSpec((1, d, d), lambda l, j: (l, 0, j)),
            pl.BlockSpec((1, 1, 1, d), lambda l, j: (l, j, 0, 0)),
        ],
        out_specs=pl.BlockSpec((1, 1, MOD_ROWS, d), lambda l, j: (l, j, 0, 0)),
        out_shape=jax.ShapeDtypeStruct((depth, n, MOD_ROWS, d), F32),
        compiler_params=_params("arbitrary", "arbitrary"),
        name="adaln",
    )(cond, mod_w, mod_b.reshape(depth, n, 1, d))
    return out.transpose(0, 2, 1, 3)


def _project_columns(items):
    project = lambda it: _dot(it[2], it[3][:, it[4] * COL_BLOCK:(it[4] + 1) * COL_BLOCK])
    z_next = project(items[0])
    for i, it in enumerate(items):
        z, z_next = z_next, (project(items[i + 1]) if i + 1 < len(items) else None)
        it[0](it[1], z)


def _rope_fns(cos, sa, sb, half):
    def tables(g):
        gb = jnp.broadcast_to(g, cos.shape)
        return gb * cos, pltpu.roll(gb, LANES - half, 1) * sa, pltpu.roll(gb, half, 1) * sb

    def rope(z, t):
        return z * t[0] + pltpu.roll(z, LANES - half, 1) * t[1] + pltpu.roll(z, half, 1) * t[2]

    return tables, rope


def _l0_proj_kernel(x_ref, ctx_ref, mod_ref, g_ref, w_ref, qg_ref, kg_ref, bd_ref, cos_ref, sa_ref, sb_ref,
                    u_ref, q_ref, k_ref, vt_ref):
    m = mod_ref[0, 0]
    h = jnp.where(pl.program_id(1) == 0, ctx_ref[0], x_ref[0])
    a = _rms_mod(h, g_ref[...], m[0:1], m[1:2]).astype(BF16)
    tables, rope = _rope_fns(cos_ref[...], sa_ref[...], sb_ref[...], DIFF_HEAD_DIM // 2)
    root_n, n_eps = DIFF_HEAD_DIM ** 0.5, DIFF_HEAD_DIM * EPS
    tq = tables(qg_ref[...] * (root_n * DIFF_HEAD_DIM ** -0.5 * LOG2E))
    tk = tables(kg_ref[...] * root_n)

    def plain(ref):
        def finish(p, z):
            ref[:, p * COL_BLOCK:(p + 1) * COL_BLOCK] = z.astype(BF16)
        return finish

    def norm_rope(ref, t):
        def finish(p, z):
            r = lax.rsqrt(_dot(jnp.square(z).astype(BF16), bd_ref[...]) + n_eps)
            for i in range(COL_BLOCK // LANES):
                sl = slice(i * LANES, (i + 1) * LANES)
                ref[:, p * COL_BLOCK + i * LANES:p * COL_BLOCK + (i + 1) * LANES] = (rope(z[:, sl], t) * r[:, sl]).astype(BF16)
        return finish

    def transposed(p, z):
        vt_ref[0, p * COL_BLOCK:(p + 1) * COL_BLOCK, :] = z.T.astype(BF16)

    items, col = [], 0
    for finish, width in ((plain(u_ref), POOL_WIDTH), (norm_rope(q_ref, tq), DIFF_WIDTH),
                          (norm_rope(k_ref, tk), DIFF_WIDTH), (transposed, DIFF_WIDTH)):
        items += [(finish, p, a, w_ref, col + p) for p in range(width // COL_BLOCK)]
        col += width // COL_BLOCK
    _project_columns(items)


def _attend_t(problems, lookahead):
    def scores(p):
        return lax.dot_general(p[0], p[1], _NT, preferred_element_type=F32)

    def finish(st, vt):
        n, dv = st.shape[0], vt.shape[0]
        kc = min(KEY_CHUNK, n)
        ones = jnp.ones((ONES_ROWS, kc), BF16)
        rs, ms = [], []
        for c in range(n // kc):
            sc = st[c * kc:(c + 1) * kc]
            ms.append(jnp.max(sc, axis=0, keepdims=True))
            e = jnp.exp2(sc - ms[-1]).astype(BF16)
            rs.append(_dot(jnp.concatenate([vt[:, c * kc:(c + 1) * kc], ones], axis=0), e))
        m_all = functools.reduce(jnp.maximum, ms)
        r = functools.reduce(jnp.add, [rc * jnp.exp2(mc - m_all) for rc, mc in zip(rs, ms)])
        return r[:dv], r[dv:dv + 1]

    outs, sts = [], []
    for i, p in enumerate(problems):
        while len(sts) < min(i + 1 + lookahead, len(problems)):
            sts.append(scores(problems[len(sts)]))
        outs.append(finish(sts[i], p[2]))
    return outs


def _diff_attn_kernel(q_ref, k_ref, vt_ref, lq1_ref, lk1_ref, lq2_ref, lk2_ref, sg_ref, o_ref, *, n_ctx, lam_init):
    j = pl.program_id(2)
    lam = (jnp.exp(jnp.sum(lq1_ref[...] * lk1_ref[...], axis=-1, keepdims=True))
           - jnp.exp(jnp.sum(lq2_ref[...] * lk2_ref[...], axis=-1, keepdims=True)) + lam_init)
    n_heads = q_ref.shape[1] // DIFF_PAIR
    lane = lax.broadcasted_iota(jnp.int32, (q_ref.shape[0], DIFF_PAIR), 1)

    def attend(n_keys):
        problems = []
        for hd in range(n_heads):
            sl = slice(hd * DIFF_PAIR, (hd + 1) * DIFF_PAIR)
            q = q_ref[:, sl].astype(F32)
            kk, vt = k_ref[0, :n_keys, sl], vt_ref[0, sl, :n_keys]
            problems.append((kk, jnp.where(lane < DIFF_HEAD_DIM, q, 0.0).astype(BF16), vt))
            problems.append((kk, jnp.where(lane >= DIFF_HEAD_DIM, q, 0.0).astype(BF16), vt))
        outs = _attend_t(problems, DIFF_LOOKAHEAD)
        for hd in range(n_heads):
            (o1, l1), (o2, l2) = outs[2 * hd], outs[2 * hd + 1]
            o = (o1 * (1.0 / l1) - o2 * (lam / l2)).T
            o_ref[:, hd * DIFF_PAIR:(hd + 1) * DIFF_PAIR] = (_rms(o, sg_ref[...]) * (1.0 - lam_init)).astype(BF16)

    @pl.when(j == 0)
    def _():
        attend(n_ctx)

    @pl.when(j > 0)
    def _():
        attend(k_ref.shape[1])


def _ffn(h, m, g_ref, wg_ref, wu_ref, wd_ref):
    a = _rms_mod(h, g_ref[...], m[3:4], m[4:5]).astype(BF16)
    gate = _dot(a, wg_ref[...])
    up = _dot(a, wu_ref[...])
    hid = (gate / (1.0 + jnp.exp(-gate)) * up).astype(BF16)
    return h + m[5:6] * _dot(hid, wd_ref[...])


def _l0_mix_ffn_kernel(x_ref, ctx_ref, u_ref, o_ref, pw_ref, ps_ref, wo_ref, mod_ref, g_ref, wg_ref, wu_ref, wd_ref,
                       out_ref, *, n_ctx, n_tok):
    tm = out_ref.shape[0]
    win = 2 * tm
    j = pl.program_id(1)
    h = jnp.where(j == 0, ctx_ref[0], x_ref[0])
    t0 = j * tm
    seg_lo = jnp.where(j == 0, 0, n_ctx)
    seg_hi = jnp.where(j == 0, n_ctx, n_tok)
    ws = jnp.where(j == 0, 0, jnp.clip(t0 - tm // 2, n_ctx, n_tok - win))
    uw = u_ref[0, pl.ds(pl.multiple_of(ws, LANES), win), :]
    ut = u_ref[0, pl.ds(pl.multiple_of(t0, tm), tm), :].astype(F32)
    row = t0 + lax.broadcasted_iota(jnp.int32, (tm, 1), 0)
    col = ws + lax.broadcasted_iota(jnp.int32, (tm, win), 1)
    ps = ps_ref[...]
    ys = []
    for g, w in enumerate(POOL_WINDOWS):
        sl = slice(g * POOL_GROUP, (g + 1) * POOL_GROUP)
        lo = jnp.maximum(row - w // 2, seg_lo)
        hi = jnp.minimum(row - w // 2 + w, seg_hi)
        band = jnp.where(col >= lo, jnp.where(col < hi, 1.0, 0.0), 0.0).astype(BF16)
        mean = _dot(band, uw[:, sl]) / (hi - lo).astype(F32)
        y = _dot((mean - ut[:, sl]).astype(BF16), pw_ref[g]) * ps[:, sl]
        ys.append(y.astype(BF16))
    y = jnp.concatenate(ys, axis=-1)
    mixed = _dot(y, wo_ref[:POOL_WIDTH, :]) + _dot(o_ref[...], wo_ref[POOL_WIDTH:, :])
    m = mod_ref[0, 0]
    out_ref[...] = _ffn(h + m[2:3] * mixed, m, g_ref, wg_ref, wu_ref, wd_ref)


def _l1_proj_kernel(h_ref, mod_ref, g_ref, wdq_ref, qlg_ref, wuq_ref, wdkv_ref, kvg_ref, wukv_ref,
                    qg_ref, kg_ref, cos_ref, sa_ref, sb_ref, ones_ref, q_ref, k_ref, vt_ref):
    m = mod_ref[0, 0]
    a = _rms_mod(h_ref[...], g_ref[...], m[0:1], m[1:2]).astype(BF16)
    tables, rope = _rope_fns(cos_ref[...], sa_ref[...], sb_ref[...], MLA_ROPE // 2)

    def head_sumsq(z):
        return _dot(jnp.square(z).astype(BF16), ones_ref[...])

    cq = _rms(_dot(a, wdq_ref[...]), qlg_ref[...]).astype(BF16)
    ckv = _dot(a, wdkv_ref[...])
    kr = ckv[:, MLA_KV_RANK:]
    ckvn = _rms(ckv[:, :MLA_KV_RANK], kvg_ref[...]).astype(BF16)
    root_n, n_eps = MLA_QK ** 0.5, MLA_QK * EPS
    tq = tables(qg_ref[...] * (root_n * MLA_QK ** -0.5 * LOG2E))
    kg = kg_ref[...] * root_n
    krr = rope(kr, tables(kg))
    kr_ss = _dot(jnp.square(kr).astype(BF16), ones_ref[:LANES, :LANES]) + n_eps
    heads = [(i, slice(i * LANES, (i + 1) * LANES)) for i in range(COL_BLOCK // LANES)]

    def q_pair(p, z):
        r = lax.rsqrt(head_sumsq(z) + n_eps)
        for i, sl in heads:
            q_ref[:, p * COL_BLOCK + i * LANES:p * COL_BLOCK + (i + 1) * LANES] = (rope(z[:, sl], tq) * r[:, sl]).astype(BF16)

    def k_pair(p, z):
        ss = head_sumsq(z)
        for i, sl in heads:
            rk = lax.rsqrt(ss[:, sl] + kr_ss)
            k_ref[:, p * COL_BLOCK + i * LANES:p * COL_BLOCK + (i + 1) * LANES] = (rk * (z[:, sl] * kg + krr)).astype(BF16)

    def v_pair(p, z):
        vt_ref[0, p * COL_BLOCK:(p + 1) * COL_BLOCK, :] = z.T.astype(BF16)

    n_kp = MLA_HEADS * LANES // COL_BLOCK
    items = [(q_pair, p, cq, wuq_ref, p) for p in range(n_kp)]
    items += [(k_pair, p, ckvn, wukv_ref, p) for p in range(n_kp)]
    items += [(v_pair, p, ckvn, wukv_ref, n_kp + p) for p in range(MLA_HEADS * MLA_V // COL_BLOCK)]
    _project_columns(items)


def _mla_attn_kernel(q_ref, k_ref, vt_ref, o_ref):
    problems = []
    for i in range(q_ref.shape[1] // LANES):
        sl = slice(i * LANES, (i + 1) * LANES)
        problems.append((k_ref[0, :, sl], q_ref[:, sl], vt_ref[0, i * MLA_V:(i + 1) * MLA_V, :]))
    outs = [ot * (1.0 / l) for ot, l in _attend_t(problems, MLA_LOOKAHEAD)]
    o_ref[...] = jnp.concatenate(outs, axis=0).T.astype(BF16)


def _l1_mix_ffn_kernel(o_ref, wo_ref, h_ref, mod_ref, g_ref, wg_ref, wu_ref, wd_ref, out_ref):
    m = mod_ref[0, 0]
    out_ref[...] = _ffn(h_ref[...] + m[2:3] * _dot(o_ref[...], wo_ref[...]), m, g_ref, wg_ref, wu_ref, wd_ref)


def _rope_tables(n_ctx, n_lat, rot_dim, first_lane):
    n_freq = rot_dim // 4
    half = rot_dim // 2
    freqs = ROPE_THETA ** (-jnp.arange(n_freq, dtype=F32) / n_freq)
    rows = n_lat // GRID_W
    row = jnp.repeat(jnp.arange(rows, dtype=F32), GRID_W)
    col = jnp.tile(jnp.arange(GRID_W, dtype=F32), rows)
    ang = jnp.concatenate([row[:, None] * freqs, col[:, None] * freqs], axis=-1)
    ang = jnp.concatenate([jnp.zeros((n_ctx, half), F32), ang], axis=0)
    cos_h, sin_h = jnp.cos(ang), jnp.sin(ang)
    zero = jnp.zeros_like(sin_h)
    n_rep = (LANES - first_lane) // rot_dim if first_lane == 0 else 1
    cos = jnp.concatenate([cos_h, cos_h] * n_rep, axis=-1)
    sa = jnp.concatenate([-sin_h, zero] * n_rep, axis=-1)
    sb = jnp.concatenate([zero, sin_h] * n_rep, axis=-1)
    n_rows = n_ctx + n_lat
    pad_lo = first_lane
    pad_hi = LANES - first_lane - cos.shape[1]
    cos = jnp.concatenate([jnp.ones((n_rows, pad_lo), F32), cos, jnp.ones((n_rows, pad_hi), F32)], axis=-1)
    sa = jnp.pad(sa, ((0, 0), (pad_lo, pad_hi)))
    sb = jnp.pad(sb, ((0, 0), (pad_lo, pad_hi)))
    return cos, sa, sb


def _pad_heads(w, width):
    k = w.shape[0]
    w = w.reshape(k, MLA_HEADS, width)
    return jnp.pad(w, ((0, 0), (0, 0), (0, LANES - width))).reshape(k, MLA_HEADS * LANES)


def kernel(x, c, ctx, c_ctx, mod_w, mod_b, norm_mix_g, norm_ffn_g, ffn_w_gate, ffn_w_up, ffn_w_down, ab_w_in, ab_w_out, pool_w, pool_scale, diff_q_norm_g, diff_k_norm_g, diff_lam_q1, diff_lam_k1, diff_lam_q2, diff_lam_k2, diff_subln_g, mla_w_dq, mla_q_lat_g, mla_w_uq, mla_w_dkv, mla_kv_lat_g, mla_w_ukv, mla_q_norm_g, mla_k_norm_g, mla_w_out):
    n_batch, n_lat, d = x.shape
    n_ctx = ctx.shape[1]
    n_tok = n_ctx + n_lat
    tm = ROW_TILE
    assert n_ctx == tm and n_lat % tm == 0 and n_lat >= 2 * tm and n_batch < MOD_ROWS
    assert mod_w.shape[0] == 2 and ab_w_in.shape[0] == 1 and mla_w_dq.shape[0] == 1
    tpb = n_tok // tm
    lpb = n_lat // tm
    n_rows = n_batch * n_tok
    ffn_hidden = ffn_w_gate.shape[-1]
    ctx_mod = n_batch

    cond = jnp.concatenate([c, c_ctx[None, :], jnp.zeros((MOD_ROWS - n_batch - 1, d), F32)], axis=0)
    mods = _adaln(cond, mod_w, mod_b)

    def lat_row(t):
        return (t // lpb) * tpb + 1 + t % lpb

    row_spec = lambda w: pl.BlockSpec((tm, w), lambda t: (t, 0))
    vec = lambda v: v.reshape(1, -1)
    mod_spec = lambda l: pl.BlockSpec((1, 1, 6, d), lambda t: (l, jnp.where(t % tpb == 0, ctx_mod, t // tpb), 0, 0))
    bt_mod = lambda l: pl.BlockSpec((1, 1, 6, d), lambda b, j: (l, jnp.where(j == 0, ctx_mod, b), 0, 0))
    lat_mod_spec = lambda l: pl.BlockSpec((1, 1, 6, d), lambda t: (l, t // lpb, 0, 0))
    bt_row = lambda w: pl.BlockSpec((tm, w), lambda b, j: (b * tpb + j, 0))
    x_spec = pl.BlockSpec((1, tm, d), lambda b, j: (b, jnp.maximum(j - 1, 0), 0))
    ctx_spec = pl.BlockSpec((1, tm, d), lambda b, j: (b, 0, 0))
    ffn_w_specs = [_const_spec((1, d)), _const_spec((d, ffn_hidden)), _const_spec((d, ffn_hidden)),
                   _const_spec((ffn_hidden, d))]

    def ffn_weights(layer):
        return (vec(norm_ffn_g[layer]), ffn_w_gate[layer].astype(BF16), ffn_w_up[layer].astype(BF16),
                ffn_w_down[layer].astype(BF16))

    lam_init = 0.8 - 0.6 * math.exp(-0.3 * 0)
    cos, sa, sb = _rope_tables(n_ctx, n_lat, DIFF_HEAD_DIM, 0)
    grp = jnp.arange(COL_BLOCK) // DIFF_HEAD_DIM
    block_diag = (grp[:, None] == grp[None, :]).astype(BF16)
    tab_spec = pl.BlockSpec((tm, LANES), lambda t: (t % tpb, 0))
    bt_tab = pl.BlockSpec((tm, LANES), lambda b, j: (j, 0))
    qkv_shape = jax.ShapeDtypeStruct((n_rows, DIFF_WIDTH), BF16)
    vt_spec = lambda w: pl.BlockSpec((1, w, tm), lambda t: (t // tpb, 0, t % tpb))
    u, q, k, vt = pl.pallas_call(
        _l0_proj_kernel,
        grid=(n_batch, tpb),
        in_specs=[x_spec, ctx_spec, bt_mod(0), _const_spec((1, d)), _const_spec(ab_w_in.shape[1:]),
                  _const_spec((1, LANES)), _const_spec((1, LANES)),
                  _const_spec((COL_BLOCK, COL_BLOCK)), bt_tab, bt_tab, bt_tab],
        out_specs=[bt_row(POOL_WIDTH), bt_row(DIFF_WIDTH), bt_row(DIFF_WIDTH),
                   pl.BlockSpec((1, DIFF_WIDTH, tm), lambda b, j: (b, 0, j))],
        out_shape=[jax.ShapeDtypeStruct((n_rows, POOL_WIDTH), BF16), qkv_shape, qkv_shape,
                   jax.ShapeDtypeStruct((n_batch, DIFF_WIDTH, n_tok), BF16)],
        compiler_params=_params("arbitrary", "arbitrary"),
        name="l0_proj",
    )(x, ctx, mods, vec(norm_mix_g[0]), ab_w_in[0].astype(BF16),
      vec(jnp.tile(diff_q_norm_g[0], LANES // DIFF_HEAD_DIM)), vec(jnp.tile(diff_k_norm_g[0], LANES // DIFF_HEAD_DIM)),
      block_diag, cos, sa, sb)

    dw = DIFF_HEADS_PER_STEP * DIFF_PAIR
    k_spec = pl.BlockSpec((1, n_tok, dw), lambda b, hd, j: (b, 0, hd))
    v_spec = pl.BlockSpec((1, dw, n_tok), lambda b, hd, j: (b, hd, 0))
    qo_spec = pl.BlockSpec((tm, dw), lambda b, hd, j: (b * tpb + j, hd))
    lam_spec = pl.BlockSpec((1, DIFF_HEAD_DIM), lambda b, hd, j: (0, 0))
    o = pl.pallas_call(
        functools.partial(_diff_attn_kernel, n_ctx=n_ctx, lam_init=lam_init),
        grid=(n_batch, DIFF_HEADS // DIFF_HEADS_PER_STEP, tpb),
        in_specs=[qo_spec, k_spec, v_spec, lam_spec, lam_spec, lam_spec, lam_spec,
                  pl.BlockSpec((1, DIFF_PAIR), lambda b, hd, j: (0, 0))],
        out_specs=qo_spec,
        out_shape=qkv_shape,
        compiler_params=_params("arbitrary", "arbitrary", "arbitrary"),
        name="diff_attn",
    )(q, k.reshape(n_batch, n_tok, DIFF_WIDTH), vt,
      vec(diff_lam_q1[0]), vec(diff_lam_k1[0]), vec(diff_lam_q2[0]), vec(diff_lam_k2[0]), vec(diff_subln_g[0]))

    h = pl.pallas_call(
        functools.partial(_l0_mix_ffn_kernel, n_ctx=n_ctx, n_tok=n_tok),
        grid=(n_batch, tpb),
        in_specs=[x_spec, ctx_spec, pl.BlockSpec((1, n_tok, POOL_WIDTH), lambda b, j: (b, 0, 0)), bt_row(DIFF_WIDTH),
                  _const_spec(pool_w.shape[1:]), _const_spec((1, POOL_WIDTH)), _const_spec(ab_w_out.shape[1:]),
                  bt_mod(0)] + ffn_w_specs,
        out_specs=bt_row(d),
        out_shape=jax.ShapeDtypeStruct((n_rows, d), F32),
        compiler_params=_params("arbitrary", "arbitrary"),
        name="l0_mix_ffn",
    )(x, ctx, u.reshape(n_batch, n_tok, POOL_WIDTH), o, pool_w[0].astype(BF16), vec(pool_scale[0]),
      ab_w_out[0].astype(BF16), mods, *ffn_weights(0))

    cos, sa, sb = _rope_tables(n_ctx, n_lat, MLA_ROPE, MLA_NOPE)
    w_uq = _pad_heads(mla_w_uq[0], MLA_QK).astype(BF16)
    w_dkv = mla_w_dkv[0]
    w_dkv = jnp.concatenate([
        w_dkv[:, :MLA_KV_RANK], jnp.zeros((d, MLA_NOPE), F32), w_dkv[:, MLA_KV_RANK:],
        jnp.zeros((d, LANES - MLA_QK), F32)], axis=-1).astype(BF16)
    w_ukv = mla_w_ukv[0].reshape(MLA_KV_RANK, MLA_HEADS, MLA_NOPE + MLA_V)
    w_ukv = jnp.concatenate([
        _pad_heads(w_ukv[:, :, :MLA_NOPE].reshape(MLA_KV_RANK, -1), MLA_NOPE),
        w_ukv[:, :, MLA_NOPE:].reshape(MLA_KV_RANK, -1)], axis=-1).astype(BF16)
    pad_gain = lambda g: vec(jnp.pad(g, (0, LANES - MLA_QK)))
    head_of = jnp.arange(2 * LANES) // LANES
    head_ones = (head_of[:, None] == head_of[None, :]).astype(BF16)
    hq = MLA_HEADS * LANES
    q, k, vt = pl.pallas_call(
        _l1_proj_kernel,
        grid=(n_rows // tm,),
        in_specs=[row_spec(d), mod_spec(1), _const_spec((1, d)), _const_spec(mla_w_dq.shape[1:]),
                  _const_spec((1, mla_w_dq.shape[2])), _const_spec(w_uq.shape), _const_spec(w_dkv.shape),
                  _const_spec((1, MLA_KV_RANK)), _const_spec(w_ukv.shape), _const_spec((1, LANES)),
                  _const_spec((1, LANES)), tab_spec, tab_spec, tab_spec, _const_spec((2 * LANES, 2 * LANES))],
        out_specs=[row_spec(hq), row_spec(hq), vt_spec(MLA_HEADS * MLA_V)],
        out_shape=[jax.ShapeDtypeStruct((n_rows, hq), BF16), jax.ShapeDtypeStruct((n_rows, hq), BF16),
                   jax.ShapeDtypeStruct((n_batch, MLA_HEADS * MLA_V, n_tok), BF16)],
        compiler_params=_params("arbitrary"),
        name="l1_proj",
    )(h, mods, vec(norm_mix_g[1]), mla_w_dq[0].astype(BF16), vec(mla_q_lat_g[0]), w_uq, w_dkv,
      vec(mla_kv_lat_g[0]), w_ukv, pad_gain(mla_q_norm_g[0]), pad_gain(mla_k_norm_g[0]), cos, sa, sb,
      head_ones)

    n_lat_rows = n_batch * n_lat
    o = pl.pallas_call(
        _mla_attn_kernel,
        grid=(n_batch, MLA_HEADS // MLA_HEADS_PER_STEP, lpb),
        in_specs=[pl.BlockSpec((tm, MLA_HEADS_PER_STEP * LANES), lambda b, hp, j: (b * tpb + 1 + j, hp)),
                  pl.BlockSpec((1, n_tok, MLA_HEADS_PER_STEP * LANES), lambda b, hp, j: (b, 0, hp)),
                  pl.BlockSpec((1, MLA_HEADS_PER_STEP * MLA_V, n_tok), lambda b, hp, j: (b, hp, 0))],
        out_specs=pl.BlockSpec((tm, MLA_HEADS_PER_STEP * MLA_V), lambda b, hp, j: (b * lpb + j, hp)),
        out_shape=jax.ShapeDtypeStruct((n_lat_rows, MLA_HEADS * MLA_V), BF16),
        compiler_params=_params("arbitrary", "arbitrary", "arbitrary"),
        name="mla_attn",
    )(q, k.reshape(n_batch, n_tok, hq), vt)

    h = pl.pallas_call(
        _l1_mix_ffn_kernel,
        grid=(n_lat_rows // tm,),
        in_specs=[row_spec(MLA_HEADS * MLA_V), _const_spec(mla_w_out.shape[1:]),
                  pl.BlockSpec((tm, d), lambda t: (lat_row(t), 0)), lat_mod_spec(1)] + ffn_w_specs,
        out_specs=row_spec(d),
        out_shape=jax.ShapeDtypeStruct((n_lat_rows, d), F32),
        compiler_params=_params("arbitrary"),
        name="l1_mix_ffn",
    )(o, mla_w_out[0].astype(BF16), h, mods, *ffn_weights(1))
    return h.reshape(n_batch, n_lat, d)
```

```python
import functools
import math

import jax
import jax.numpy as jnp
from jax import lax
from jax.experimental import pallas as pl
from jax.experimental.pallas import tpu as pltpu

F32 = jnp.float32
BF16 = jnp.bfloat16

EPS = 1e-6
GRID_W = 64
ROPE_THETA = 10000.0
POOL_WINDOWS = (2, 4, 8, 16)
POOL_GROUP = 128
POOL_WIDTH = POOL_GROUP * len(POOL_WINDOWS)
DIFF_HEADS = 4
DIFF_HEAD_DIM = 64
DIFF_PAIR = 2 * DIFF_HEAD_DIM
DIFF_WIDTH = DIFF_HEADS * DIFF_PAIR
MLA_HEADS = 16
MLA_NOPE = 64
MLA_ROPE = 32
MLA_QK = MLA_NOPE + MLA_ROPE
MLA_V = 64
MLA_KV_RANK = 256
LANES = 128
ROW_TILE = 256
COL_BLOCK = 256
MOD_ROWS = 16
ONES_ROWS = 16
LOG2E = math.log2(math.e)
DIFF_HEADS_PER_STEP = 4
MLA_HEADS_PER_STEP = 16
KEY_CHUNK = 256
PROJECT_LOOKAHEAD = 3
DIFF_LOOKAHEAD = 3
MLA_LOOKAHEAD = 1
VMEM_LIMIT = 56 * 1024 * 1024

_NT = (((1,), (1,)), ((), ()))


def _dot(a, b):
    return jnp.dot(a, b, preferred_element_type=F32)


def _rms(x, g):
    return x * lax.rsqrt(jnp.mean(x * x, axis=-1, keepdims=True) + EPS) * g


def _rms_mod(x, g, shift, scale):
    return x * lax.rsqrt(jnp.mean(x * x, axis=-1, keepdims=True) + EPS) * (g * (1.0 + scale)) + shift


def _params(*sem):
    return pltpu.CompilerParams(dimension_semantics=sem, vmem_limit_bytes=VMEM_LIMIT)


def _const_spec(shape):
    zeros = (0,) * len(shape)
    return pl.BlockSpec(shape, lambda *_: zeros, pipeline_mode=pl.Buffered(1))


def _adaln_kernel(cond_ref, w_ref, b_ref, o_ref):
    c = cond_ref[...]
    a = (c / (1.0 + jnp.exp(-c))).astype(BF16)
    o_ref[0, 0] = _dot(a, w_ref[0].astype(BF16)) + b_ref[0, 0]


def _adaln(cond, mod_w, mod_b):
    depth, d, d6 = mod_w.shape
    n = d6 // d
    out = pl.pallas_call(
        _adaln_kernel,
        grid=(depth, n),
        in_specs=[
            pl.BlockSpec((MOD_ROWS, d), lambda l, j: (0, 0)),
            pl.BlockSpec((1, d, d), lambda l, j: (l, 0, j)),
            pl.BlockSpec((1, 1, 1, d), lambda l, j: (l, j, 0, 0)),
        ],
        out_specs=pl.BlockSpec((1, 1, MOD_ROWS, d), lambda l, j: (l, j, 0, 0)),
        out_shape=jax.ShapeDtypeStruct((depth, n, MOD_ROWS, d), F32),
        compiler_params=_params("arbitrary", "arbitrary"),
        name="adaln",
    )(cond, mod_w, mod_b.reshape(depth, n, 1, d))
    return out.transpose(0, 2, 1, 3)


def _project_columns(items):
    project = lambda it: _dot(it[2], it[3][:, it[4] * COL_BLOCK:(it[4] + 1) * COL_BLOCK])
    zs = []
    for i, it in enumerate(items):
        while len(zs) < min(i + 1 + PROJECT_LOOKAHEAD, len(items)):
            zs.append(project(items[len(zs)]))
        it[0](it[1], zs[i])
        zs[i] = None


def _rope_fns(cos, sa, sb, half):
    def tables(g):
        gb = jnp.broadcast_to(g, cos.shape)
        return gb * cos, pltpu.roll(gb, LANES - half, 1) * sa, pltpu.roll(gb, half, 1) * sb

    def rope(z, t):
        return z * t[0] + pltpu.roll(z, LANES - half, 1) * t[1] + pltpu.roll(z, half, 1) * t[2]

    return tables, rope


def _l0_proj_kernel(x_ref, ctx_ref, mod_ref, g_ref, w_ref, qg_ref, kg_ref, bd_ref, cos_ref, sa_ref, sb_ref,
                    u_ref, q_ref, k_ref, vt_ref):
    m = mod_ref[0, 0]
    h = jnp.where(pl.program_id(1) == 0, ctx_ref[0], x_ref[0])
    a = _rms_mod(h, g_ref[...], m[0:1], m[1:2]).astype(BF16)
    tables, rope = _rope_fns(cos_ref[...], sa_ref[...], sb_ref[...], DIFF_HEAD_DIM // 2)
    root_n, n_eps = DIFF_HEAD_DIM ** 0.5, DIFF_HEAD_DIM * EPS
    tq = tables(qg_ref[...] * (root_n * DIFF_HEAD_DIM ** -0.5 * LOG2E))
    tk = tables(kg_ref[...] * root_n)

    def plain(ref):
        def finish(p, z):
            ref[:, p * COL_BLOCK:(p + 1) * COL_BLOCK] = z.astype(BF16)
        return finish

    def norm_rope(ref, t):
        def finish(p, z):
            r = lax.rsqrt(_dot(jnp.square(z).astype(BF16), bd_ref[...]) + n_eps)
            for i in range(COL_BLOCK // LANES):
                sl = slice(i * LANES, (i + 1) * LANES)
                ref[:, p * COL_BLOCK + i * LANES:p * COL_BLOCK + (i + 1) * LANES] = (rope(z[:, sl], t) * r[:, sl]).astype(BF16)
        return finish

    def transposed(p, z):
        vt_ref[0, p * COL_BLOCK:(p + 1) * COL_BLOCK, :] = z.T.astype(BF16)

    items, col = [], 0
    for finish, width in ((plain(u_ref), POOL_WIDTH), (norm_rope(q_ref, tq), DIFF_WIDTH),
                          (norm_rope(k_ref, tk), DIFF_WIDTH), (transposed, DIFF_WIDTH)):
        items += [(finish, p, a, w_ref, col + p) for p in range(width // COL_BLOCK)]
        col += width // COL_BLOCK
    _project_columns(items)


def _attend_t(problems, lookahead):
    def scores(p):
        return lax.dot_general(p[0], p[1], _NT, preferred_element_type=F32)

    def finish(st, vt):
        n, dv = st.shape[0], vt.shape[0]
        kc = min(KEY_CHUNK, n)
        ones = jnp.ones((ONES_ROWS, kc), BF16)
        rs, ms = [], []
        for c in range(n // kc):
            sc = st[c * kc:(c + 1) * kc]
            ms.append(jnp.max(sc, axis=0, keepdims=True))
            e = jnp.exp2(sc - ms[-1]).astype(BF16)
            rs.append(_dot(jnp.concatenate([vt[:, c * kc:(c + 1) * kc], ones], axis=0), e))
        m_all = functools.reduce(jnp.maximum, ms)
        r = functools.reduce(jnp.add, [rc * jnp.exp2(mc - m_all) for rc, mc in zip(rs, ms)])
        return r[:dv], r[dv:dv + 1]

    outs, sts = [], []
    for i, p in enumerate(problems):
        while len(sts) < min(i + 1 + lookahead, len(problems)):
            sts.append(scores(problems[len(sts)]))
        outs.append(finish(sts[i], p[2]))
    return outs


def _diff_attn_kernel(q_ref, k_ref, vt_ref, lq1_ref, lk1_ref, lq2_ref, lk2_ref, sg_ref, o_ref, *, n_ctx, lam_init):
    j = pl.program_id(2)
    lam = (jnp.exp(jnp.sum(lq1_ref[...] * lk1_ref[...], axis=-1, keepdims=True))
           - jnp.exp(jnp.sum(lq2_ref[...] * lk2_ref[...], axis=-1, keepdims=True)) + lam_init)
    n_heads = q_ref.shape[1] // DIFF_PAIR
    lane = lax.broadcasted_iota(jnp.int32, (q_ref.shape[0], DIFF_PAIR), 1)

    def attend(n_keys):
        problems = []
        for hd in range(n_heads):
            sl = slice(hd * DIFF_PAIR, (hd + 1) * DIFF_PAIR)
            q = q_ref[:, sl].astype(F32)
            kk, vt = k_ref[0, :n_keys, sl], vt_ref[0, sl, :n_keys]
            problems.append((kk, jnp.where(lane < DIFF_HEAD_DIM, q, 0.0).astype(BF16), vt))
            problems.append((kk, jnp.where(lane >= DIFF_HEAD_DIM, q, 0.0).astype(BF16), vt))
        outs = _attend_t(problems, DIFF_LOOKAHEAD)
        for hd in range(n_heads):
            (o1, l1), (o2, l2) = outs[2 * hd], outs[2 * hd + 1]
            o = (o1 * (1.0 / l1) - o2 * (lam / l2)).T
            o_ref[:, hd * DIFF_PAIR:(hd + 1) * DIFF_PAIR] = (_rms(o, sg_ref[...]) * (1.0 - lam_init)).astype(BF16)

    @pl.when(j == 0)
    def _():
        attend(n_ctx)

    @pl.when(j > 0)
    def _():
        attend(k_ref.shape[1])


def _ffn(h, m, g_ref, wg_ref, wu_ref, wd_ref):
    a = _rms_mod(h, g_ref[...], m[3:4], m[4:5]).astype(BF16)
    gate = _dot(a, wg_ref[...])
    up = _dot(a, wu_ref[...])
    hid = (gate / (1.0 + jnp.exp(-gate)) * up).astype(BF16)
    return h + m[5:6] * _dot(hid, wd_ref[...])


def _l0_mix_ffn_kernel(x_ref, ctx_ref, u_ref, o_ref, pw_ref, ps_ref, wo_ref, mod_ref, g_ref, wg_ref, wu_ref, wd_ref,
                       out_ref, *, n_ctx, n_tok):
    tm = out_ref.shape[0]
    win = 2 * tm
    j = pl.program_id(1)
    h = jnp.where(j == 0, ctx_ref[0], x_ref[0])
    t0 = j * tm
    seg_lo = jnp.where(j == 0, 0, n_ctx)
    seg_hi = jnp.where(j == 0, n_ctx, n_tok)
    ws = jnp.where(j == 0, 0, jnp.clip(t0 - tm // 2, n_ctx, n_tok - win))
    uw = u_ref[0, pl.ds(pl.multiple_of(ws, LANES), win), :]
    ut = u_ref[0, pl.ds(pl.multiple_of(t0, tm), tm), :].astype(F32)
    row = t0 + lax.broadcasted_iota(jnp.int32, (tm, 1), 0)
    col = ws + lax.broadcasted_iota(jnp.int32, (tm, win), 1)
    ps = ps_ref[...]
    ys = []
    for g, w in enumerate(POOL_WINDOWS):
        sl = slice(g * POOL_GROUP, (g + 1) * POOL_GROUP)
        lo = jnp.maximum(row - w // 2, seg_lo)
        hi = jnp.minimum(row - w // 2 + w, seg_hi)
        band = jnp.where(col >= lo, jnp.where(col < hi, 1.0, 0.0), 0.0).astype(BF16)
        mean = _dot(band, uw[:, sl]) / (hi - lo).astype(F32)
        y = _dot((mean - ut[:, sl]).astype(BF16), pw_ref[g]) * ps[:, sl]
        ys.append(y.astype(BF16))
    y = jnp.concatenate(ys, axis=-1)
    mixed = _dot(y, wo_ref[:POOL_WIDTH, :]) + _dot(o_ref[...], wo_ref[POOL_WIDTH:, :])
    m = mod_ref[0, 0]
    out_ref[...] = _ffn(h + m[2:3] * mixed, m, g_ref, wg_ref, wu_ref, wd_ref)


def _l1_proj_kernel(h_ref, mod_ref, g_ref, wdq_ref, qlg_ref, wuq_ref, wdkv_ref, kvg_ref, wukv_ref,
                    qg_ref, kg_ref, cos_ref, sa_ref, sb_ref, ones_ref, q_ref, k_ref, vt_ref):
    m = mod_ref[0, 0]
    a = _rms_mod(h_ref[...], g_ref[...], m[0:1], m[1:2]).astype(BF16)
    tables, rope = _rope_fns(cos_ref[...], sa_ref[...], sb_ref[...], MLA_ROPE // 2)

    def head_sumsq(z):
        return _dot(jnp.square(z).astype(BF16), ones_ref[...])

    cq = _rms(_dot(a, wdq_ref[...]), qlg_ref[...]).astype(BF16)
    ckv = _dot(a, wdkv_ref[...])
    kr = ckv[:, MLA_KV_RANK:]
    ckvn = _rms(ckv[:, :MLA_KV_RANK], kvg_ref[...]).astype(BF16)
    root_n, n_eps = MLA_QK ** 0.5, MLA_QK * EPS
    tq = tables(qg_ref[...] * (root_n * MLA_QK ** -0.5 * LOG2E))
    kg = kg_ref[...] * root_n
    krr = rope(kr, tables(kg))
    kr_ss = _dot(jnp.square(kr).astype(BF16), ones_ref[:LANES, :LANES]) + n_eps
    heads = [(i, slice(i * LANES, (i + 1) * LANES)) for i in range(COL_BLOCK // LANES)]

    def q_pair(p, z):
        r = lax.rsqrt(head_sumsq(z) + n_eps)
        for i, sl in heads:
            q_ref[:, p * COL_BLOCK + i * LANES:p * COL_BLOCK + (i + 1) * LANES] = (rope(z[:, sl], tq) * r[:, sl]).astype(BF16)

    def k_pair(p, z):
        ss = head_sumsq(z)
        for i, sl in heads:
            rk = lax.rsqrt(ss[:, sl] + kr_ss)
            k_ref[:, p * COL_BLOCK + i * LANES:p * COL_BLOCK + (i + 1) * LANES] = (rk * (z[:, sl] * kg + krr)).astype(BF16)

    def v_pair(p, z):
        vt_ref[0, p * COL_BLOCK:(p + 1) * COL_BLOCK, :] = z.T.astype(BF16)

    n_kp = MLA_HEADS * LANES // COL_BLOCK
    n_vp = MLA_HEADS * MLA_V // COL_BLOCK
    items = []
    for p in range(n_kp):
        items += [(q_pair, p, cq, wuq_ref, p), (k_pair, p, ckvn, wukv_ref, p)]
        if p % (n_kp // n_vp) == 0:
            items.append((v_pair, p // (n_kp // n_vp), ckvn, wukv_ref, n_kp + p // (n_kp // n_vp)))
    _project_columns(items)


def _mla_attn_kernel(q_ref, k_ref, vt_ref, o_ref):
    problems = []
    for i in range(q_ref.shape[1] // LANES):
        sl = slice(i * LANES, (i + 1) * LANES)
        problems.append((k_ref[0, :, sl], q_ref[:, sl], vt_ref[0, i * MLA_V:(i + 1) * MLA_V, :]))
    outs = [ot * (1.0 / l) for ot, l in _attend_t(problems, MLA_LOOKAHEAD)]
    o_ref[...] = jnp.concatenate(outs, axis=0).T.astype(BF16)


def _l1_mix_ffn_kernel(o_ref, wo_ref, h_ref, mod_ref, g_ref, wg_ref, wu_ref, wd_ref, out_ref):
    m = mod_ref[0, 0]
    out_ref[...] = _ffn(h_ref[...] + m[2:3] * _dot(o_ref[...], wo_ref[...]), m, g_ref, wg_ref, wu_ref, wd_ref)


def _rope_tables(n_ctx, n_lat, rot_dim, first_lane):
    n_freq = rot_dim // 4
    half = rot_dim // 2
    freqs = ROPE_THETA ** (-jnp.arange(n_freq, dtype=F32) / n_freq)
    rows = n_lat // GRID_W
    row = jnp.repeat(jnp.arange(rows, dtype=F32), GRID_W)
    col = jnp.tile(jnp.arange(GRID_W, dtype=F32), rows)
    ang = jnp.concatenate([row[:, None] * freqs, col[:, None] * freqs], axis=-1)
    ang = jnp.concatenate([jnp.zeros((n_ctx, half), F32), ang], axis=0)
    cos_h, sin_h = jnp.cos(ang), jnp.sin(ang)
    zero = jnp.zeros_like(sin_h)
    n_rep = (LANES - first_lane) // rot_dim if first_lane == 0 else 1
    cos = jnp.concatenate([cos_h, cos_h] * n_rep, axis=-1)
    sa = jnp.concatenate([-sin_h, zero] * n_rep, axis=-1)
    sb = jnp.concatenate([zero, sin_h] * n_rep, axis=-1)
    n_rows = n_ctx + n_lat
    pad_lo = first_lane
    pad_hi = LANES - first_lane - cos.shape[1]
    cos = jnp.concatenate([jnp.ones((n_rows, pad_lo), F32), cos, jnp.ones((n_rows, pad_hi), F32)], axis=-1)
    sa = jnp.pad(sa, ((0, 0), (pad_lo, pad_hi)))
    sb = jnp.pad(sb, ((0, 0), (pad_lo, pad_hi)))
    return cos, sa, sb


def _pad_heads(w, width):
    k = w.shape[0]
    w = w.reshape(k, MLA_HEADS, width)
    return jnp.pad(w, ((0, 0), (0, 0), (0, LANES - width))).reshape(k, MLA_HEADS * LANES)


def kernel(x, c, ctx, c_ctx, mod_w, mod_b, norm_mix_g, norm_ffn_g, ffn_w_gate, ffn_w_up, ffn_w_down, ab_w_in, ab_w_out, pool_w, pool_scale, diff_q_norm_g, diff_k_norm_g, diff_lam_q1, diff_lam_k1, diff_lam_q2, diff_lam_k2, diff_subln_g, mla_w_dq, mla_q_lat_g, mla_w_uq, mla_w_dkv, mla_kv_lat_g, mla_w_ukv, mla_q_norm_g, mla_k_norm_g, mla_w_out):
    n_batch, n_lat, d = x.shape
    n_ctx = ctx.shape[1]
    n_tok = n_ctx + n_lat
    tm = ROW_TILE
    assert n_ctx == tm and n_lat % tm == 0 and n_lat >= 2 * tm and n_batch < MOD_ROWS
    assert mod_w.shape[0] == 2 and ab_w_in.shape[0] == 1 and mla_w_dq.shape[0] == 1
    tpb = n_tok // tm
    lpb = n_lat // tm
    n_rows = n_batch * n_tok
    ffn_hidden = ffn_w_gate.shape[-1]
    ctx_mod = n_batch

    cond = jnp.concatenate([c, c_ctx[None, :], jnp.zeros((MOD_ROWS - n_batch - 1, d), F32)], axis=0)
    mods = _adaln(cond, mod_w, mod_b)

    def lat_row(t):
        return (t // lpb) * tpb + 1 + t % lpb

    row_spec = lambda w: pl.BlockSpec((tm, w), lambda t: (t, 0))
    vec = lambda v: v.reshape(1, -1)
    mod_spec = lambda l: pl.BlockSpec((1, 1, 6, d), lambda t: (l, jnp.where(t % tpb == 0, ctx_mod, t // tpb), 0, 0))
    bt_mod = lambda l: pl.BlockSpec((1, 1, 6, d), lambda b, j: (l, jnp.where(j == 0, ctx_mod, b), 0, 0))
    lat_mod_spec = lambda l: pl.BlockSpec((1, 1, 6, d), lambda t: (l, t // lpb, 0, 0))
    bt_row = lambda w: pl.BlockSpec((tm, w), lambda b, j: (b * tpb + j, 0))
    x_spec = pl.BlockSpec((1, tm, d), lambda b, j: (b, jnp.maximum(j - 1, 0), 0))
    ctx_spec = pl.BlockSpec((1, tm, d), lambda b, j: (b, 0, 0))
    ffn_w_specs = [_const_spec((1, d)), _const_spec((d, ffn_hidden)), _const_spec((d, ffn_hidden)),
                   _const_spec((ffn_hidden, d))]

    def ffn_weights(layer):
        return (vec(norm_ffn_g[layer]), ffn_w_gate[layer].astype(BF16), ffn_w_up[layer].astype(BF16),
                ffn_w_down[layer].astype(BF16))

    lam_init = 0.8 - 0.6 * math.exp(-0.3 * 0)
    cos, sa, sb = _rope_tables(n_ctx, n_lat, DIFF_HEAD_DIM, 0)
    grp = jnp.arange(COL_BLOCK) // DIFF_HEAD_DIM
    block_diag = (grp[:, None] == grp[None, :]).astype(BF16)
    tab_spec = pl.BlockSpec((tm, LANES), lambda t: (t % tpb, 0))
    bt_tab = pl.BlockSpec((tm, LANES), lambda b, j: (j, 0))
    qkv_shape = jax.ShapeDtypeStruct((n_rows, DIFF_WIDTH), BF16)
    vt_spec = lambda w: pl.BlockSpec((1, w, tm), lambda t: (t // tpb, 0, t % tpb))
    u, q, k, vt = pl.pallas_call(
        _l0_proj_kernel,
        grid=(n_batch, tpb),
        in_specs=[x_spec, ctx_spec, bt_mod(0), _const_spec((1, d)), _const_spec(ab_w_in.shape[1:]),
                  _const_spec((1, LANES)), _const_spec((1, LANES)),
                  _const_spec((COL_BLOCK, COL_BLOCK)), bt_tab, bt_tab, bt_tab],
        out_specs=[bt_row(POOL_WIDTH), bt_row(DIFF_WIDTH), bt_row(DIFF_WIDTH),
                   pl.BlockSpec((1, DIFF_WIDTH, tm), lambda b, j: (b, 0, j))],
        out_shape=[jax.ShapeDtypeStruct((n_rows, POOL_WIDTH), BF16), qkv_shape, qkv_shape,
                   jax.ShapeDtypeStruct((n_batch, DIFF_WIDTH, n_tok), BF16)],
        compiler_params=_params("arbitrary", "arbitrary"),
        name="l0_proj",
    )(x, ctx, mods, vec(norm_mix_g[0]), ab_w_in[0].astype(BF16),
      vec(jnp.tile(diff_q_norm_g[0], LANES // DIFF_HEAD_DIM)), vec(jnp.tile(diff_k_norm_g[0], LANES // DIFF_HEAD_DIM)),
      block_diag, cos, sa, sb)

    dw = DIFF_HEADS_PER_STEP * DIFF_PAIR
    k_spec = pl.BlockSpec((1, n_tok, dw), lambda b, hd, j: (b, 0, hd))
    v_spec = pl.BlockSpec((1, dw, n_tok), lambda b, hd, j: (b, hd, 0))
    qo_spec = pl.BlockSpec((tm, dw), lambda b, hd, j: (b * tpb + j, hd))
    lam_spec = pl.BlockSpec((1, DIFF_HEAD_DIM), lambda b, hd, j: (0, 0))
    o = pl.pallas_call(
        functools.partial(_diff_attn_kernel, n_ctx=n_ctx, lam_init=lam_init),
        grid=(n_batch, DIFF_HEADS // DIFF_HEADS_PER_STEP, tpb),
        in_specs=[qo_spec, k_spec, v_spec, lam_spec, lam_spec, lam_spec, lam_spec,
                  pl.BlockSpec((1, DIFF_PAIR), lambda b, hd, j: (0, 0))],
        out_specs=qo_spec,
        out_shape=qkv_shape,
        compiler_params=_params("arbitrary", "arbitrary", "arbitrary"),
        name="diff_attn",
    )(q, k.reshape(n_batch, n_tok, DIFF_WIDTH), vt,
      vec(diff_lam_q1[0]), vec(diff_lam_k1[0]), vec(diff_lam_q2[0]), vec(diff_lam_k2[0]), vec(diff_subln_g[0]))

    h = pl.pallas_call(
        functools.partial(_l0_mix_ffn_kernel, n_ctx=n_ctx, n_tok=n_tok),
        grid=(n_batch, tpb),
        in_specs=[x_spec, ctx_spec, pl.BlockSpec((1, n_tok, POOL_WIDTH), lambda b, j: (b, 0, 0)), bt_row(DIFF_WIDTH),
                  _const_spec(pool_w.shape[1:]), _const_spec((1, POOL_WIDTH)), _const_spec(ab_w_out.shape[1:]),
                  bt_mod(0)] + ffn_w_specs,
        out_specs=bt_row(d),
        out_shape=jax.ShapeDtypeStruct((n_rows, d), F32),
        compiler_params=_params("arbitrary", "arbitrary"),
        name="l0_mix_ffn",
    )(x, ctx, u.reshape(n_batch, n_tok, POOL_WIDTH), o, pool_w[0].astype(BF16), vec(pool_scale[0]),
      ab_w_out[0].astype(BF16), mods, *ffn_weights(0))

    cos, sa, sb = _rope_tables(n_ctx, n_lat, MLA_ROPE, MLA_NOPE)
    w_uq = _pad_heads(mla_w_uq[0], MLA_QK).astype(BF16)
    w_dkv = mla_w_dkv[0]
    w_dkv = jnp.concatenate([
        w_dkv[:, :MLA_KV_RANK], jnp.zeros((d, MLA_NOPE), F32), w_dkv[:, MLA_KV_RANK:],
        jnp.zeros((d, LANES - MLA_QK), F32)], axis=-1).astype(BF16)
    w_ukv = mla_w_ukv[0].reshape(MLA_KV_RANK, MLA_HEADS, MLA_NOPE + MLA_V)
    w_ukv = jnp.concatenate([
        _pad_heads(w_ukv[:, :, :MLA_NOPE].reshape(MLA_KV_RANK, -1), MLA_NOPE),
        w_ukv[:, :, MLA_NOPE:].reshape(MLA_KV_RANK, -1)], axis=-1).astype(BF16)
    pad_gain = lambda g: vec(jnp.pad(g, (0, LANES - MLA_QK)))
    head_of = jnp.arange(2 * LANES) // LANES
    head_ones = (head_of[:, None] == head_of[None, :]).astype(BF16)
    hq = MLA_HEADS * LANES
    q, k, vt = pl.pallas_call(
        _l1_proj_kernel,
        grid=(n_rows // tm,),
        in_specs=[row_spec(d), mod_spec(1), _const_spec((1, d)), _const_spec(mla_w_dq.shape[1:]),
                  _const_spec((1, mla_w_dq.shape[2])), _const_spec(w_uq.shape), _const_spec(w_dkv.shape),
                  _const_spec((1, MLA_KV_RANK)), _const_spec(w_ukv.shape), _const_spec((1, LANES)),
                  _const_spec((1, LANES)), tab_spec, tab_spec, tab_spec, _const_spec((2 * LANES, 2 * LANES))],
        out_specs=[row_spec(hq), row_spec(hq), vt_spec(MLA_HEADS * MLA_V)],
        out_shape=[jax.ShapeDtypeStruct((n_rows, hq), BF16), jax.ShapeDtypeStruct((n_rows, hq), BF16),
                   jax.ShapeDtypeStruct((n_batch, MLA_HEADS * MLA_V, n_tok), BF16)],
        compiler_params=_params("arbitrary"),
        name="l1_proj",
    )(h, mods, vec(norm_mix_g[1]), mla_w_dq[0].astype(BF16), vec(mla_q_lat_g[0]), w_uq, w_dkv,
      vec(mla_kv_lat_g[0]), w_ukv, pad_gain(mla_q_norm_g[0]), pad_gain(mla_k_norm_g[0]), cos, sa, sb,
      head_ones)

    n_lat_rows = n_batch * n_lat
    o = pl.pallas_call(
        _mla_attn_kernel,
        grid=(n_batch, MLA_HEADS // MLA_HEADS_PER_STEP, lpb),
        in_specs=[pl.BlockSpec((tm, MLA_HEADS_PER_STEP * LANES), lambda b, hp, j: (b * tpb + 1 + j, hp)),
                  pl.BlockSpec((1, n_tok, MLA_HEADS_PER_STEP * LANES), lambda b, hp, j: (b, 0, hp)),
                  pl.BlockSpec((1, MLA_HEADS_PER_STEP * MLA_V, n_tok), lambda b, hp, j: (b, hp, 0))],
        out_specs=pl.BlockSpec((tm, MLA_HEADS_PER_STEP * MLA_V), lambda b, hp, j: (b * lpb + j, hp)),
        out_shape=jax.ShapeDtypeStruct((n_lat_rows, MLA_HEADS * MLA_V), BF16),
        compiler_params=_params("arbitrary", "arbitrary", "arbitrary"),
        name="mla_attn",
    )(q, k.reshape(n_batch, n_tok, hq), vt)

    h = pl.pallas_call(
        _l1_mix_ffn_kernel,
        grid=(n_lat_rows // tm,),
        in_specs=[row_spec(MLA_HEADS * MLA_V), _const_spec(mla_w_out.shape[1:]),
                  pl.BlockSpec((tm, d), lambda t: (lat_row(t), 0)), lat_mod_spec(1)] + ffn_w_specs,
        out_specs=row_spec(d),
        out_shape=jax.ShapeDtypeStruct((n_lat_rows, d), F32),
        compiler_params=_params("arbitrary"),
        name="l1_mix_ffn",
    )(o, mla_w_out[0].astype(BF16), h, mods, *ffn_weights(1))
    return h.reshape(n_batch, n_lat, d)
```

```python
import functools
import math

import jax
import jax.numpy as jnp
from jax import lax
from jax.experimental import pallas as pl
from jax.experimental.pallas import tpu as pltpu

F32 = jnp.float32
BF16 = jnp.bfloat16

EPS = 1e-6
GRID_W = 64
ROPE_THETA = 10000.0
POOL_WINDOWS = (2, 4, 8, 16)
POOL_GROUP = 128
POOL_WIDTH = POOL_GROUP * len(POOL_WINDOWS)
DIFF_HEADS = 4
DIFF_HEAD_DIM = 64
DIFF_PAIR = 2 * DIFF_HEAD_DIM
DIFF_WIDTH = DIFF_HEADS * DIFF_PAIR
MLA_HEADS = 16
MLA_NOPE = 64
MLA_ROPE = 32
MLA_QK = MLA_NOPE + MLA_ROPE
MLA_V = 64
MLA_KV_RANK = 256
LANES = 128
ROW_TILE = 256
COL_BLOCK = 256
MOD_ROWS = 16
ONES_ROWS = 16
LOG2E = math.log2(math.e)
DIFF_HEADS_PER_STEP = 4
MLA_HEADS_PER_STEP = 16
KEY_CHUNK = 256
PROJECT_LOOKAHEAD = 3
DIFF_LOOKAHEAD = 3
MLA_LOOKAHEAD = 1
VMEM_LIMIT = 56 * 1024 * 1024

_NT = (((1,), (1,)), ((), ()))


def _dot(a, b):
    return jnp.dot(a, b, preferred_element_type=F32)


def _rms(x, g):
    return x * lax.rsqrt(jnp.mean(x * x, axis=-1, keepdims=True) + EPS) * g


def _rms_mod(x, g, shift, scale):
    return x * lax.rsqrt(jnp.mean(x * x, axis=-1, keepdims=True) + EPS) * (g * (1.0 + scale)) + shift


def _params(*sem):
    return pltpu.CompilerParams(dimension_semantics=sem, vmem_limit_bytes=VMEM_LIMIT)


def _const_spec(shape):
    zeros = (0,) * len(shape)
    return pl.BlockSpec(shape, lambda *_: zeros, pipeline_mode=pl.Buffered(1))


def _adaln_kernel(cond_ref, w_ref, b_ref, o_ref):
    c = cond_ref[...]
    a = (c / (1.0 + jnp.exp(-c))).astype(BF16)
    o_ref[0, 0] = _dot(a, w_ref[0].astype(BF16)) + b_ref[0, 0]


def _adaln(cond, mod_w, mod_b):
    depth, d, d6 = mod_w.shape
    n = d6 // d
    out = pl.pallas_call(
        _adaln_kernel,
        grid=(depth, n),
        in_specs=[
            pl.BlockSpec((MOD_ROWS, d), lambda l, j: (0, 0)),
            pl.BlockSpec((1, d, d), lambda l, j: (l, 0, j)),
            pl.BlockSpec((1, 1, 1, d), lambda l, j: (l, j, 0, 0)),
        ],
        out_specs=pl.BlockSpec((1, 1, MOD_ROWS, d), lambda l, j: (l, j, 0, 0)),
        out_shape=jax.ShapeDtypeStruct((depth, n, MOD_ROWS, d), F32),
        compiler_params=_params("arbitrary", "arbitrary"),
        name="adaln",
    )(cond, mod_w, mod_b.reshape(depth, n, 1, d))
    return out.transpose(0, 2, 1, 3)


def _project_columns(items):
    project = lambda it: _dot(it[2], it[3][:, it[4] * COL_BLOCK:(it[4] + 1) * COL_BLOCK])
    zs = []
    for i, it in enumerate(items):
        while len(zs) < min(i + 1 + PROJECT_LOOKAHEAD, len(items)):
            zs.append(project(items[len(zs)]))
        it[0](it[1], zs[i])
        zs[i] = None


def _rope_fns(cos, sa, sb, half):
    def tables(g):
        gb = jnp.broadcast_to(g, cos.shape)
        return gb * cos, pltpu.roll(gb, LANES - half, 1) * sa, pltpu.roll(gb, half, 1) * sb

    def rope(z, t):
        return z * t[0] + pltpu.roll(z, LANES - half, 1) * t[1] + pltpu.roll(z, half, 1) * t[2]

    return tables, rope


def _l0_proj_kernel(x_ref, ctx_ref, mod_ref, g_ref, w_ref, qg_ref, kg_ref, bd_ref, cos_ref, sa_ref, sb_ref,
                    u_ref, q_ref, k_ref, vt_ref):
    m = mod_ref[0, 0]
    h = jnp.where(pl.program_id(1) == 0, ctx_ref[0], x_ref[0])
    a = _rms_mod(h, g_ref[...], m[0:1], m[1:2]).astype(BF16)
    tables, rope = _rope_fns(cos_ref[...], sa_ref[...], sb_ref[...], DIFF_HEAD_DIM // 2)
    root_n, n_eps = DIFF_HEAD_DIM ** 0.5, DIFF_HEAD_DIM * EPS
    tq = tables(qg_ref[...] * (root_n * DIFF_HEAD_DIM ** -0.5 * LOG2E))
    tk = tables(kg_ref[...] * root_n)

    def plain(ref):
        def finish(p, z):
            ref[:, p * COL_BLOCK:(p + 1) * COL_BLOCK] = z.astype(BF16)
        return finish

    def norm_rope(ref, t):
        def finish(p, z):
            r = lax.rsqrt(_dot(jnp.square(z).astype(BF16), bd_ref[...]) + n_eps)
            for i in range(COL_BLOCK // LANES):
                sl = slice(i * LANES, (i + 1) * LANES)
                ref[:, p * COL_BLOCK + i * LANES:p * COL_BLOCK + (i + 1) * LANES] = (rope(z[:, sl], t) * r[:, sl]).astype(BF16)
        return finish

    def transposed(p, z):
        vt_ref[0, p * COL_BLOCK:(p + 1) * COL_BLOCK, :] = z.T.astype(BF16)

    items, col = [], 0
    for finish, width in ((plain(u_ref), POOL_WIDTH), (norm_rope(q_ref, tq), DIFF_WIDTH),
                          (norm_rope(k_ref, tk), DIFF_WIDTH), (transposed, DIFF_WIDTH)):
        items += [(finish, p, a, w_ref, col + p) for p in range(width // COL_BLOCK)]
        col += width // COL_BLOCK
    _project_columns(items)


def _attend_t(problems, lookahead):
    def scores(p):
        return lax.dot_general(p[0], p[1], _NT, preferred_element_type=F32)

    def finish(st, vt):
        n, dv = st.shape[0], vt.shape[0]
        rs, ms = [], []
        for lo in range(0, n, KEY_CHUNK):
            hi = min(lo + KEY_CHUNK, n)
            sc = st[lo:hi]
            ms.append(jnp.max(sc, axis=0, keepdims=True))
            e = jnp.exp2(sc - ms[-1]).astype(BF16)
            lhs = jnp.concatenate([vt[:, lo:hi], jnp.ones((ONES_ROWS, hi - lo), BF16)], axis=0)
            rs.append(_dot(lhs, e))
        m_all = functools.reduce(jnp.maximum, ms)
        r = functools.reduce(jnp.add, [rc * jnp.exp2(mc - m_all) for rc, mc in zip(rs, ms)])
        return r[:dv], r[dv:dv + 1]

    outs, sts = [], []
    for i, p in enumerate(problems):
        while len(sts) < min(i + 1 + lookahead, len(problems)):
            sts.append(scores(problems[len(sts)]))
        outs.append(finish(sts[i], p[2]))
    return outs


def _diff_attn_kernel(q_ref, k_ref, vt_ref, lq1_ref, lk1_ref, lq2_ref, lk2_ref, sg_ref, o_ref, *, n_ctx, lam_init):
    j = pl.program_id(2)
    lam = (jnp.exp(jnp.sum(lq1_ref[...] * lk1_ref[...], axis=-1, keepdims=True))
           - jnp.exp(jnp.sum(lq2_ref[...] * lk2_ref[...], axis=-1, keepdims=True)) + lam_init)
    n_heads = q_ref.shape[1] // DIFF_PAIR
    lane = lax.broadcasted_iota(jnp.int32, (q_ref.shape[0], DIFF_PAIR), 1)

    def attend(n_keys):
        problems = []
        for hd in range(n_heads):
            sl = slice(hd * DIFF_PAIR, (hd + 1) * DIFF_PAIR)
            q = q_ref[:, sl].astype(F32)
            kk, vt = k_ref[0, :n_keys, sl], vt_ref[0, sl, :n_keys]
            problems.append((kk, jnp.where(lane < DIFF_HEAD_DIM, q, 0.0).astype(BF16), vt))
            problems.append((kk, jnp.where(lane >= DIFF_HEAD_DIM, q, 0.0).astype(BF16), vt))
        outs = _attend_t(problems, DIFF_LOOKAHEAD)
        for hd in range(n_heads):
            (o1, l1), (o2, l2) = outs[2 * hd], outs[2 * hd + 1]
            o = (o1 * (1.0 / l1) - o2 * (lam / l2)).T
            o_ref[:, hd * DIFF_PAIR:(hd + 1) * DIFF_PAIR] = (_rms(o, sg_ref[...]) * (1.0 - lam_init)).astype(BF16)

    @pl.when(j == 0)
    def _():
        attend(n_ctx)

    @pl.when(j > 0)
    def _():
        attend(k_ref.shape[1])


def _ffn(h, m, g_ref, wg_ref, wu_ref, wd_ref):
    a = _rms_mod(h, g_ref[...], m[3:4], m[4:5]).astype(BF16)
    gate = _dot(a, wg_ref[...])
    up = _dot(a, wu_ref[...])
    hid = (gate / (1.0 + jnp.exp(-gate)) * up).astype(BF16)
    return h + m[5:6] * _dot(hid, wd_ref[...])


def _l0_mix_ffn_kernel(x_ref, ctx_ref, u_ref, o_ref, pw_ref, ps_ref, wo_ref, mod_ref, g_ref, wg_ref, wu_ref, wd_ref,
                       out_ref, *, n_ctx, n_tok):
    tm = out_ref.shape[0]
    win = 2 * tm
    j = pl.program_id(1)
    h = jnp.where(j == 0, ctx_ref[0], x_ref[0])
    t0 = j * tm
    seg_lo = jnp.where(j == 0, 0, n_ctx)
    seg_hi = jnp.where(j == 0, n_ctx, n_tok)
    ws = jnp.where(j == 0, 0, jnp.clip(t0 - tm // 2, n_ctx, n_tok - win))
    uw = u_ref[0, pl.ds(pl.multiple_of(ws, LANES), win), :]
    ut = u_ref[0, pl.ds(pl.multiple_of(t0, tm), tm), :].astype(F32)
    row = t0 + lax.broadcasted_iota(jnp.int32, (tm, 1), 0)
    col = ws + lax.broadcasted_iota(jnp.int32, (tm, win), 1)
    ps = ps_ref[...]
    groups = [slice(g * POOL_GROUP, (g + 1) * POOL_GROUP) for g in range(len(POOL_WINDOWS))]
    attn_part = _dot(o_ref[...], wo_ref[POOL_WIDTH:, :])
    means = []
    for sl, w in zip(groups, POOL_WINDOWS):
        lo = jnp.maximum(row - w // 2, seg_lo)
        hi = jnp.minimum(row - w // 2 + w, seg_hi)
        band = jnp.where(col >= lo, jnp.where(col < hi, 1.0, 0.0), 0.0).astype(BF16)
        means.append(_dot(band, uw[:, sl]) / (hi - lo).astype(F32))
    ys = [(_dot((mean - ut[:, sl]).astype(BF16), pw_ref[g]) * ps[:, sl]).astype(BF16)
          for g, (sl, mean) in enumerate(zip(groups, means))]
    mixed = _dot(jnp.concatenate(ys, axis=-1), wo_ref[:POOL_WIDTH, :]) + attn_part
    m = mod_ref[0, 0]
    out_ref[...] = _ffn(h + m[2:3] * mixed, m, g_ref, wg_ref, wu_ref, wd_ref)


def _l1_proj_kernel(h_ref, mod_ref, g_ref, wdq_ref, qlg_ref, wuq_ref, wdkv_ref, kvg_ref, wukv_ref,
                    qg_ref, kg_ref, cos_ref, sa_ref, sb_ref, ones_ref, q_ref, k_ref, vt_ref):
    m = mod_ref[0, 0]
    a = _rms_mod(h_ref[...], g_ref[...], m[0:1], m[1:2]).astype(BF16)
    tables, rope = _rope_fns(cos_ref[...], sa_ref[...], sb_ref[...], MLA_ROPE // 2)

    def head_sumsq(z):
        return _dot(jnp.square(z).astype(BF16), ones_ref[...])

    cq = _rms(_dot(a, wdq_ref[...]), qlg_ref[...]).astype(BF16)
    ckv = _dot(a, wdkv_ref[...])
    kr = ckv[:, MLA_KV_RANK:]
    ckvn = _rms(ckv[:, :MLA_KV_RANK], kvg_ref[...]).astype(BF16)
    root_n, n_eps = MLA_QK ** 0.5, MLA_QK * EPS
    tq = tables(qg_ref[...] * (root_n * MLA_QK ** -0.5 * LOG2E))
    kg = kg_ref[...] * root_n
    krr = rope(kr, tables(kg))
    kr_ss = _dot(jnp.square(kr).astype(BF16), ones_ref[:LANES, :LANES]) + n_eps
    heads = [(i, slice(i * LANES, (i + 1) * LANES)) for i in range(COL_BLOCK // LANES)]

    def q_pair(p, z):
        r = lax.rsqrt(head_sumsq(z) + n_eps)
        for i, sl in heads:
            q_ref[:, p * COL_BLOCK + i * LANES:p * COL_BLOCK + (i + 1) * LANES] = (rope(z[:, sl], tq) * r[:, sl]).astype(BF16)

    def k_pair(p, z):
        ss = head_sumsq(z)
        for i, sl in heads:
            rk = lax.rsqrt(ss[:, sl] + kr_ss)
            k_ref[:, p * COL_BLOCK + i * LANES:p * COL_BLOCK + (i + 1) * LANES] = (rk * (z[:, sl] * kg + krr)).astype(BF16)

    def v_pair(p, z):
        vt_ref[0, p * COL_BLOCK:(p + 1) * COL_BLOCK, :] = z.T.astype(BF16)

    n_kp = MLA_HEADS * LANES // COL_BLOCK
    n_vp = MLA_HEADS * MLA_V // COL_BLOCK
    items = []
    for p in range(n_kp):
        items += [(q_pair, p, cq, wuq_ref, p), (k_pair, p, ckvn, wukv_ref, p)]
        if p % (n_kp // n_vp) == 0:
            items.append((v_pair, p // (n_kp // n_vp), ckvn, wukv_ref, n_kp + p // (n_kp // n_vp)))
    _project_columns(items)


def _mla_attn_kernel(q_ref, k_ref, vt_ref, o_ref):
    problems = []
    for i in range(q_ref.shape[1] // LANES):
        sl = slice(i * LANES, (i + 1) * LANES)
        problems.append((k_ref[0, :, sl], q_ref[:, sl], vt_ref[0, i * MLA_V:(i + 1) * MLA_V, :]))
    outs = [ot * (1.0 / l) for ot, l in _attend_t(problems, MLA_LOOKAHEAD)]
    o_ref[...] = jnp.concatenate(outs, axis=0).T.astype(BF16)


def _l1_mix_ffn_kernel(o_ref, wo_ref, h_ref, mod_ref, g_ref, wg_ref, wu_ref, wd_ref, out_ref):
    m = mod_ref[0, 0]
    out_ref[...] = _ffn(h_ref[...] + m[2:3] * _dot(o_ref[...], wo_ref[...]), m, g_ref, wg_ref, wu_ref, wd_ref)


def _rope_tables(n_ctx, n_lat, rot_dim, first_lane):
    n_freq = rot_dim // 4
    half = rot_dim // 2
    freqs = ROPE_THETA ** (-jnp.arange(n_freq, dtype=F32) / n_freq)
    rows = n_lat // GRID_W
    row = jnp.repeat(jnp.arange(rows, dtype=F32), GRID_W)
    col = jnp.tile(jnp.arange(GRID_W, dtype=F32), rows)
    ang = jnp.concatenate([row[:, None] * freqs, col[:, None] * freqs], axis=-1)
    ang = jnp.concatenate([jnp.zeros((n_ctx, half), F32), ang], axis=0)
    cos_h, sin_h = jnp.cos(ang), jnp.sin(ang)
    zero = jnp.zeros_like(sin_h)
    n_rep = (LANES - first_lane) // rot_dim if first_lane == 0 else 1
    cos = jnp.concatenate([cos_h, cos_h] * n_rep, axis=-1)
    sa = jnp.concatenate([-sin_h, zero] * n_rep, axis=-1)
    sb = jnp.concatenate([zero, sin_h] * n_rep, axis=-1)
    n_rows = n_ctx + n_lat
    pad_lo = first_lane
    pad_hi = LANES - first_lane - cos.shape[1]
    cos = jnp.concatenate([jnp.ones((n_rows, pad_lo), F32), cos, jnp.ones((n_rows, pad_hi), F32)], axis=-1)
    sa = jnp.pad(sa, ((0, 0), (pad_lo, pad_hi)))
    sb = jnp.pad(sb, ((0, 0), (pad_lo, pad_hi)))
    return cos, sa, sb


def _pad_heads(w, width):
    k = w.shape[0]
    w = w.reshape(k, MLA_HEADS, width)
    return jnp.pad(w, ((0, 0), (0, 0), (0, LANES - width))).reshape(k, MLA_HEADS * LANES)


def kernel(x, c, ctx, c_ctx, mod_w, mod_b, norm_mix_g, norm_ffn_g, ffn_w_gate, ffn_w_up, ffn_w_down, ab_w_in, ab_w_out, pool_w, pool_scale, diff_q_norm_g, diff_k_norm_g, diff_lam_q1, diff_lam_k1, diff_lam_q2, diff_lam_k2, diff_subln_g, mla_w_dq, mla_q_lat_g, mla_w_uq, mla_w_dkv, mla_kv_lat_g, mla_w_ukv, mla_q_norm_g, mla_k_norm_g, mla_w_out):
    n_batch, n_lat, d = x.shape
    n_ctx = ctx.shape[1]
    n_tok = n_ctx + n_lat
    tm = ROW_TILE
    assert n_ctx == tm and n_lat % tm == 0 and n_lat >= 2 * tm and n_batch < MOD_ROWS
    assert mod_w.shape[0] == 2 and ab_w_in.shape[0] == 1 and mla_w_dq.shape[0] == 1
    tpb = n_tok // tm
    lpb = n_lat // tm
    n_rows = n_batch * n_tok
    ffn_hidden = ffn_w_gate.shape[-1]
    ctx_mod = n_batch

    cond = jnp.concatenate([c, c_ctx[None, :], jnp.zeros((MOD_ROWS - n_batch - 1, d), F32)], axis=0)
    mods = _adaln(cond, mod_w, mod_b)

    def lat_row(t):
        return (t // lpb) * tpb + 1 + t % lpb

    row_spec = lambda w: pl.BlockSpec((tm, w), lambda t: (t, 0))
    vec = lambda v: v.reshape(1, -1)
    mod_spec = lambda l: pl.BlockSpec((1, 1, 6, d), lambda t: (l, jnp.where(t % tpb == 0, ctx_mod, t // tpb), 0, 0))
    bt_mod = lambda l: pl.BlockSpec((1, 1, 6, d), lambda b, j: (l, jnp.where(j == 0, ctx_mod, b), 0, 0))
    lat_mod_spec = lambda l: pl.BlockSpec((1, 1, 6, d), lambda t: (l, t // lpb, 0, 0))
    bt_row = lambda w: pl.BlockSpec((tm, w), lambda b, j: (b * tpb + j, 0))
    x_spec = pl.BlockSpec((1, tm, d), lambda b, j: (b, jnp.maximum(j - 1, 0), 0))
    ctx_spec = pl.BlockSpec((1, tm, d), lambda b, j: (b, 0, 0))
    ffn_w_specs = [_const_spec((1, d)), _const_spec((d, ffn_hidden)), _const_spec((d, ffn_hidden)),
                   _const_spec((ffn_hidden, d))]

    def ffn_weights(layer):
        return (vec(norm_ffn_g[layer]), ffn_w_gate[layer].astype(BF16), ffn_w_up[layer].astype(BF16),
                ffn_w_down[layer].astype(BF16))

    lam_init = 0.8 - 0.6 * math.exp(-0.3 * 0)
    cos, sa, sb = _rope_tables(n_ctx, n_lat, DIFF_HEAD_DIM, 0)
    grp = jnp.arange(COL_BLOCK) // DIFF_HEAD_DIM
    block_diag = (grp[:, None] == grp[None, :]).astype(BF16)
    tab_spec = pl.BlockSpec((tm, LANES), lambda t: (t % tpb, 0))
    bt_tab = pl.BlockSpec((tm, LANES), lambda b, j: (j, 0))
    qkv_shape = jax.ShapeDtypeStruct((n_rows, DIFF_WIDTH), BF16)
    vt_spec = lambda w: pl.BlockSpec((1, w, tm), lambda t: (t // tpb, 0, t % tpb))
    u, q, k, vt = pl.pallas_call(
        _l0_proj_kernel,
        grid=(n_batch, tpb),
        in_specs=[x_spec, ctx_spec, bt_mod(0), _const_spec((1, d)), _const_spec(ab_w_in.shape[1:]),
                  _const_spec((1, LANES)), _const_spec((1, LANES)),
                  _const_spec((COL_BLOCK, COL_BLOCK)), bt_tab, bt_tab, bt_tab],
        out_specs=[bt_row(POOL_WIDTH), bt_row(DIFF_WIDTH), bt_row(DIFF_WIDTH),
                   pl.BlockSpec((1, DIFF_WIDTH, tm), lambda b, j: (b, 0, j))],
        out_shape=[jax.ShapeDtypeStruct((n_rows, POOL_WIDTH), BF16), qkv_shape, qkv_shape,
                   jax.ShapeDtypeStruct((n_batch, DIFF_WIDTH, n_tok), BF16)],
        compiler_params=_params("arbitrary", "arbitrary"),
        name="l0_proj",
    )(x, ctx, mods, vec(norm_mix_g[0]), ab_w_in[0].astype(BF16),
      vec(jnp.tile(diff_q_norm_g[0], LANES // DIFF_HEAD_DIM)), vec(jnp.tile(diff_k_norm_g[0], LANES // DIFF_HEAD_DIM)),
      block_diag, cos, sa, sb)

    dw = DIFF_HEADS_PER_STEP * DIFF_PAIR
    k_spec = pl.BlockSpec((1, n_tok, dw), lambda b, hd, j: (b, 0, hd))
    v_spec = pl.BlockSpec((1, dw, n_tok), lambda b, hd, j: (b, hd, 0))
    qo_spec = pl.BlockSpec((tm, dw), lambda b, hd, j: (b * tpb + j, hd))
    lam_spec = pl.BlockSpec((1, DIFF_HEAD_DIM), lambda b, hd, j: (0, 0))
    o = pl.pallas_call(
        functools.partial(_diff_attn_kernel, n_ctx=n_ctx, lam_init=lam_init),
        grid=(n_batch, DIFF_HEADS // DIFF_HEADS_PER_STEP, tpb),
        in_specs=[qo_spec, k_spec, v_spec, lam_spec, lam_spec, lam_spec, lam_spec,
                  pl.BlockSpec((1, DIFF_PAIR), lambda b, hd, j: (0, 0))],
        out_specs=qo_spec,
        out_shape=qkv_shape,
        compiler_params=_params("arbitrary", "arbitrary", "arbitrary"),
        name="diff_attn",
    )(q, k.reshape(n_batch, n_tok, DIFF_WIDTH), vt,
      vec(diff_lam_q1[0]), vec(diff_lam_k1[0]), vec(diff_lam_q2[0]), vec(diff_lam_k2[0]), vec(diff_subln_g[0]))

    h = pl.pallas_call(
        functools.partial(_l0_mix_ffn_kernel, n_ctx=n_ctx, n_tok=n_tok),
        grid=(n_batch, tpb),
        in_specs=[x_spec, ctx_spec, pl.BlockSpec((1, n_tok, POOL_WIDTH), lambda b, j: (b, 0, 0)), bt_row(DIFF_WIDTH),
                  _const_spec(pool_w.shape[1:]), _const_spec((1, POOL_WIDTH)), _const_spec(ab_w_out.shape[1:]),
                  bt_mod(0)] + ffn_w_specs,
        out_specs=bt_row(d),
        out_shape=jax.ShapeDtypeStruct((n_rows, d), F32),
        compiler_params=_params("arbitrary", "arbitrary"),
        name="l0_mix_ffn",
    )(x, ctx, u.reshape(n_batch, n_tok, POOL_WIDTH), o, pool_w[0].astype(BF16), vec(pool_scale[0]),
      ab_w_out[0].astype(BF16), mods, *ffn_weights(0))

    cos, sa, sb = _rope_tables(n_ctx, n_lat, MLA_ROPE, MLA_NOPE)
    w_uq = _pad_heads(mla_w_uq[0], MLA_QK).astype(BF16)
    w_dkv = mla_w_dkv[0]
    w_dkv = jnp.concatenate([
        w_dkv[:, :MLA_KV_RANK], jnp.zeros((d, MLA_NOPE), F32), w_dkv[:, MLA_KV_RANK:],
        jnp.zeros((d, LANES - MLA_QK), F32)], axis=-1).astype(BF16)
    w_ukv = mla_w_ukv[0].reshape(MLA_KV_RANK, MLA_HEADS, MLA_NOPE + MLA_V)
    w_ukv = jnp.concatenate([
        _pad_heads(w_ukv[:, :, :MLA_NOPE].reshape(MLA_KV_RANK, -1), MLA_NOPE),
        w_ukv[:, :, MLA_NOPE:].reshape(MLA_KV_RANK, -1)], axis=-1).astype(BF16)
    pad_gain = lambda g: vec(jnp.pad(g, (0, LANES - MLA_QK)))
    head_of = jnp.arange(2 * LANES) // LANES
    head_ones = (head_of[:, None] == head_of[None, :]).astype(BF16)
    hq = MLA_HEADS * LANES
    q, k, vt = pl.pallas_call(
        _l1_proj_kernel,
        grid=(n_rows // tm,),
        in_specs=[row_spec(d), mod_spec(1), _const_spec((1, d)), _const_spec(mla_w_dq.shape[1:]),
                  _const_spec((1, mla_w_dq.shape[2])), _const_spec(w_uq.shape), _const_spec(w_dkv.shape),
                  _const_spec((1, MLA_KV_RANK)), _const_spec(w_ukv.shape), _const_spec((1, LANES)),
                  _const_spec((1, LANES)), tab_spec, tab_spec, tab_spec, _const_spec((2 * LANES, 2 * LANES))],
        out_specs=[row_spec(hq), row_spec(hq), vt_spec(MLA_HEADS * MLA_V)],
        out_shape=[jax.ShapeDtypeStruct((n_rows, hq), BF16), jax.ShapeDtypeStruct((n_rows, hq), BF16),
                   jax.ShapeDtypeStruct((n_batch, MLA_HEADS * MLA_V, n_tok), BF16)],
        compiler_params=_params("arbitrary"),
        name="l1_proj",
    )(h, mods, vec(norm_mix_g[1]), mla_w_dq[0].astype(BF16), vec(mla_q_lat_g[0]), w_uq, w_dkv,
      vec(mla_kv_lat_g[0]), w_ukv, pad_gain(mla_q_norm_g[0]), pad_gain(mla_k_norm_g[0]), cos, sa, sb,
      head_ones)

    n_lat_rows = n_batch * n_lat
    o = pl.pallas_call(
        _mla_attn_kernel,
        grid=(n_batch, MLA_HEADS // MLA_HEADS_PER_STEP, lpb),
        in_specs=[pl.BlockSpec((tm, MLA_HEADS_PER_STEP * LANES), lambda b, hp, j: (b * tpb + 1 + j, hp)),
                  pl.BlockSpec((1, n_tok, MLA_HEADS_PER_STEP * LANES), lambda b, hp, j: (b, 0, hp)),
                  pl.BlockSpec((1, MLA_HEADS_PER_STEP * MLA_V, n_tok), lambda b, hp, j: (b, hp, 0))],
        out_specs=pl.BlockSpec((tm, MLA_HEADS_PER_STEP * MLA_V), lambda b, hp, j: (b * lpb + j, hp)),
        out_shape=jax.ShapeDtypeStruct((n_lat_rows, MLA_HEADS * MLA_V), BF16),
        compiler_params=_params("arbitrary", "arbitrary", "arbitrary"),
        name="mla_attn",
    )(q, k.reshape(n_batch, n_tok, hq), vt)

    h = pl.pallas_call(
        _l1_mix_ffn_kernel,
        grid=(n_lat_rows // tm,),
        in_specs=[row_spec(MLA_HEADS * MLA_V), _const_spec(mla_w_out.shape[1:]),
                  pl.BlockSpec((tm, d), lambda t: (lat_row(t), 0)), lat_mod_spec(1)] + ffn_w_specs,
        out_specs=row_spec(d),
        out_shape=jax.ShapeDtypeStruct((n_lat_rows, d), F32),
        compiler_params=_params("arbitrary"),
        name="l1_mix_ffn",
    )(o, mla_w_out[0].astype(BF16), h, mods, *ffn_weights(1))
    return h.reshape(n_batch, n_lat, d)
```

```python
import functools
import math

import jax
import jax.numpy as jnp
from jax import lax
from jax.experimental import pallas as pl
from jax.experimental.pallas import tpu as pltpu

F32 = jnp.float32
BF16 = jnp.bfloat16

EPS = 1e-6
GRID_W = 64
ROPE_THETA = 10000.0
POOL_WINDOWS = (2, 4, 8, 16)
POOL_GROUP = 128
POOL_WIDTH = POOL_GROUP * len(POOL_WINDOWS)
DIFF_HEADS = 4
DIFF_HEAD_DIM = 64
DIFF_PAIR = 2 * DIFF_HEAD_DIM
DIFF_WIDTH = DIFF_HEADS * DIFF_PAIR
MLA_HEADS = 16
MLA_NOPE = 64
MLA_ROPE = 32
MLA_QK = MLA_NOPE + MLA_ROPE
MLA_V = 64
MLA_KV_RANK = 256
LANES = 128
ROW_TILE = 256
COL_BLOCK = 256
PROJ_SUBTILES = 3
FFN_SUBTILES = 2
MOD_ROWS = 16
ONES_ROWS = 16
LOG2E = math.log2(math.e)
DIFF_HEADS_PER_STEP = 4
MLA_HEADS_PER_STEP = 16
KEY_CHUNK = 256
PROJECT_LOOKAHEAD = 3
DIFF_LOOKAHEAD = 3
MLA_LOOKAHEAD = 1
VMEM_LIMIT = 56 * 1024 * 1024

_NT = (((1,), (1,)), ((), ()))


def _dot(a, b):
    return jnp.dot(a, b, preferred_element_type=F32)


def _rms(x, g):
    return x * lax.rsqrt(jnp.mean(x * x, axis=-1, keepdims=True) + EPS) * g


def _rms_mod(x, g, shift, scale):
    return x * lax.rsqrt(jnp.mean(x * x, axis=-1, keepdims=True) + EPS) * (g * (1.0 + scale)) + shift


def _params(*sem):
    return pltpu.CompilerParams(dimension_semantics=sem, vmem_limit_bytes=VMEM_LIMIT)


def _const_spec(shape):
    zeros = (0,) * len(shape)
    return pl.BlockSpec(shape, lambda *_: zeros, pipeline_mode=pl.Buffered(1))


def _adaln_kernel(cond_ref, w_ref, b_ref, o_ref):
    c = cond_ref[...]
    a = (c / (1.0 + jnp.exp(-c))).astype(BF16)
    o_ref[0, 0] = _dot(a, w_ref[0].astype(BF16)) + b_ref[0, 0]


def _adaln(cond, mod_w, mod_b):
    depth, d, d6 = mod_w.shape
    n = d6 // d
    out = pl.pallas_call(
        _adaln_kernel,
        grid=(depth, n),
        in_specs=[
            pl.BlockSpec((MOD_ROWS, d), lambda l, j: (0, 0)),
            pl.BlockSpec((1, d, d), lambda l, j: (l, 0, j)),
            pl.BlockSpec((1, 1, 1, d), lambda l, j: (l, j, 0, 0)),
        ],
        out_specs=pl.BlockSpec((1, 1, MOD_ROWS, d), lambda l, j: (l, j, 0, 0)),
        out_shape=jax.ShapeDtypeStruct((depth, n, MOD_ROWS, d), F32),
        compiler_params=_params("arbitrary", "arbitrary"),
        name="adaln",
    )(cond, mod_w, mod_b.reshape(depth, n, 1, d))
    return out.transpose(0, 2, 1, 3)


def _project_columns(items):
    project = lambda it: _dot(it[2], it[3][:, it[4] * COL_BLOCK:(it[4] + 1) * COL_BLOCK])
    zs = []
    for i, it in enumerate(items):
        while len(zs) < min(i + 1 + PROJECT_LOOKAHEAD, len(items)):
            zs.append(project(items[len(zs)]))
        it[0](it[1], zs[i])
        zs[i] = None


def _rope_fns(cos, sa, sb, half):
    def tables(g):
        gb = jnp.broadcast_to(g, cos.shape)
        return gb * cos, pltpu.roll(gb, LANES - half, 1) * sa, pltpu.roll(gb, half, 1) * sb

    def rope(z, t):
        return z * t[0] + pltpu.roll(z, LANES - half, 1) * t[1] + pltpu.roll(z, half, 1) * t[2]

    return tables, rope


def _l0_proj_kernel(*refs):
    x_refs, refs = refs[:PROJ_SUBTILES], refs[PROJ_SUBTILES:]
    (ctx_ref, mod_ref, cmod_ref, g_ref, w_ref, qg_ref, kg_ref, bd_ref, cos_ref, sa_ref, sb_ref,
     u_ref, q_ref, k_ref, vt_ref) = refs
    subs = []
    for s, x_ref in enumerate(x_refs):
        h, m = x_ref[0], mod_ref[0, 0]
        if s == 0:
            is_ctx = pl.program_id(1) == 0
            h, m = jnp.where(is_ctx, ctx_ref[0], h), jnp.where(is_ctx, cmod_ref[0, 0], m)
        subs.append(_rms_mod(h, g_ref[...], m[0:1], m[1:2]).astype(BF16))
    a = jnp.concatenate(subs, axis=0)
    tables, rope = _rope_fns(cos_ref[...], sa_ref[...], sb_ref[...], DIFF_HEAD_DIM // 2)
    root_n, n_eps = DIFF_HEAD_DIM ** 0.5, DIFF_HEAD_DIM * EPS
    tq = tables(qg_ref[...] * (root_n * DIFF_HEAD_DIM ** -0.5 * LOG2E))
    tk = tables(kg_ref[...] * root_n)

    def plain(ref):
        def finish(p, z):
            ref[:, p * COL_BLOCK:(p + 1) * COL_BLOCK] = z.astype(BF16)
        return finish

    def norm_rope(ref, t):
        def finish(p, z):
            r = lax.rsqrt(_dot(jnp.square(z).astype(BF16), bd_ref[...]) + n_eps)
            for i in range(COL_BLOCK // LANES):
                sl = slice(i * LANES, (i + 1) * LANES)
                ref[:, p * COL_BLOCK + i * LANES:p * COL_BLOCK + (i + 1) * LANES] = (rope(z[:, sl], t) * r[:, sl]).astype(BF16)
        return finish

    def transposed(p, z):
        vt_ref[0, p * COL_BLOCK:(p + 1) * COL_BLOCK, :] = z.T.astype(BF16)

    items, col = [], 0
    for finish, width in ((plain(u_ref), POOL_WIDTH), (norm_rope(q_ref, tq), DIFF_WIDTH),
                          (norm_rope(k_ref, tk), DIFF_WIDTH), (transposed, DIFF_WIDTH)):
        items += [(finish, p, a, w_ref, col + p) for p in range(width // COL_BLOCK)]
        col += width // COL_BLOCK
    _project_columns(items)


def _attend_t(problems, lookahead):
    def scores(p):
        return lax.dot_general(p[0], p[1], _NT, preferred_element_type=F32)

    def finish(st, vt):
        n, dv = st.shape[0], vt.shape[0]
        rs, ms = [], []
        for lo in range(0, n, KEY_CHUNK):
            hi = min(lo + KEY_CHUNK, n)
            sc = st[lo:hi]
            ms.append(jnp.max(sc, axis=0, keepdims=True))
            e = jnp.exp2(sc - ms[-1]).astype(BF16)
            lhs = jnp.concatenate([vt[:, lo:hi], jnp.ones((ONES_ROWS, hi - lo), BF16)], axis=0)
            rs.append(_dot(lhs, e))
        m_all = functools.reduce(jnp.maximum, ms)
        r = functools.reduce(jnp.add, [rc * jnp.exp2(mc - m_all) for rc, mc in zip(rs, ms)])
        return r[:dv], r[dv:dv + 1]

    outs, sts = [], []
    for i, p in enumerate(problems):
        while len(sts) < min(i + 1 + lookahead, len(problems)):
            sts.append(scores(problems[len(sts)]))
        outs.append(finish(sts[i], p[2]))
    return outs


def _diff_attn_kernel(q_ref, k_ref, vt_ref, lq1_ref, lk1_ref, lq2_ref, lk2_ref, sg_ref, o_ref, *, n_ctx, lam_init):
    j = pl.program_id(2)
    lam = (jnp.exp(jnp.sum(lq1_ref[...] * lk1_ref[...], axis=-1, keepdims=True))
           - jnp.exp(jnp.sum(lq2_ref[...] * lk2_ref[...], axis=-1, keepdims=True)) + lam_init)
    n_heads = q_ref.shape[1] // DIFF_PAIR
    lane = lax.broadcasted_iota(jnp.int32, (q_ref.shape[0], DIFF_PAIR), 1)

    def attend(n_keys):
        problems = []
        for hd in range(n_heads):
            sl = slice(hd * DIFF_PAIR, (hd + 1) * DIFF_PAIR)
            q = q_ref[:, sl].astype(F32)
            kk, vt = k_ref[0, :n_keys, sl], vt_ref[0, sl, :n_keys]
            problems.append((kk, jnp.where(lane < DIFF_HEAD_DIM, q, 0.0).astype(BF16), vt))
            problems.append((kk, jnp.where(lane >= DIFF_HEAD_DIM, q, 0.0).astype(BF16), vt))
        outs = _attend_t(problems, DIFF_LOOKAHEAD)
        for hd in range(n_heads):
            (o1, l1), (o2, l2) = outs[2 * hd], outs[2 * hd + 1]
            o = (o1 * (1.0 / l1) - o2 * (lam / l2)).T
            o_ref[:, hd * DIFF_PAIR:(hd + 1) * DIFF_PAIR] = (_rms(o, sg_ref[...]) * (1.0 - lam_init)).astype(BF16)

    @pl.when(j == 0)
    def _():
        attend(n_ctx)

    @pl.when(j > 0)
    def _():
        attend(k_ref.shape[1])


def _ffn(h, m, g_ref, wg_ref, wu_ref, wd_ref):
    a = _rms_mod(h, g_ref[...], m[3:4], m[4:5]).astype(BF16)
    gate = _dot(a, wg_ref[...])
    up = _dot(a, wu_ref[...])
    hid = (gate / (1.0 + jnp.exp(-gate)) * up).astype(BF16)
    return h + m[5:6] * _dot(hid, wd_ref[...])


def _l0_mix_ffn_kernel(x_ref, ctx_ref, u_ref, o_ref, pw_ref, ps_ref, wo_ref, mod_ref, g_ref, wg_ref, wu_ref, wd_ref,
                       out_ref, *, n_ctx, n_tok):
    tm = out_ref.shape[0]
    win = 2 * tm
    j = pl.program_id(1)
    h = jnp.where(j == 0, ctx_ref[0], x_ref[0])
    t0 = j * tm
    seg_lo = jnp.where(j == 0, 0, n_ctx)
    seg_hi = jnp.where(j == 0, n_ctx, n_tok)
    ws = jnp.where(j == 0, 0, jnp.clip(t0 - tm // 2, n_ctx, n_tok - win))
    uw = u_ref[0, pl.ds(pl.multiple_of(ws, LANES), win), :]
    ut = u_ref[0, pl.ds(pl.multiple_of(t0, tm), tm), :].astype(F32)
    row = t0 + lax.broadcasted_iota(jnp.int32, (tm, 1), 0)
    col = ws + lax.broadcasted_iota(jnp.int32, (tm, win), 1)
    ps = ps_ref[...]
    groups = [slice(g * POOL_GROUP, (g + 1) * POOL_GROUP) for g in range(len(POOL_WINDOWS))]
    attn_part = _dot(o_ref[...], wo_ref[POOL_WIDTH:, :])
    means = []
    for sl, w in zip(groups, POOL_WINDOWS):
        lo = jnp.maximum(row - w // 2, seg_lo)
        hi = jnp.minimum(row - w // 2 + w, seg_hi)
        band = jnp.where(col >= lo, jnp.where(col < hi, 1.0, 0.0), 0.0).astype(BF16)
        means.append(_dot(band, uw[:, sl]) / (hi - lo).astype(F32))
    ys = [(_dot((mean - ut[:, sl]).astype(BF16), pw_ref[g]) * ps[:, sl]).astype(BF16)
          for g, (sl, mean) in enumerate(zip(groups, means))]
    mixed = _dot(jnp.concatenate(ys, axis=-1), wo_ref[:POOL_WIDTH, :]) + attn_part
    m = mod_ref[0, 0]
    out_ref[...] = _ffn(h + m[2:3] * mixed, m, g_ref, wg_ref, wu_ref, wd_ref)


def _l1_proj_kernel(h_ref, mod_ref, cmod_ref, g_ref, wdq_ref, qlg_ref, wuq_ref, wdkv_ref, kvg_ref, wukv_ref,
                    qg_ref, kg_ref, cos_ref, sa_ref, sb_ref, ones_ref, q_ref, k_ref, vt_ref):
    subs = []
    for s in range(h_ref.shape[0] // ROW_TILE):
        is_ctx = (pl.program_id(1) == 0) if s == 0 else False
        m = jnp.where(is_ctx, cmod_ref[0, 0], mod_ref[0, 0]) if s == 0 else mod_ref[0, 0]
        subs.append(_rms_mod(h_ref[s * ROW_TILE:(s + 1) * ROW_TILE, :], g_ref[...], m[0:1], m[1:2]).astype(BF16))
    a = jnp.concatenate(subs, axis=0)
    tables, rope = _rope_fns(cos_ref[...], sa_ref[...], sb_ref[...], MLA_ROPE // 2)

    def head_sumsq(z):
        return _dot(jnp.square(z).astype(BF16), ones_ref[...])

    cq = _rms(_dot(a, wdq_ref[...]), qlg_ref[...]).astype(BF16)
    ckv = _dot(a, wdkv_ref[...])
    kr = ckv[:, MLA_KV_RANK:]
    ckvn = _rms(ckv[:, :MLA_KV_RANK], kvg_ref[...]).astype(BF16)
    root_n, n_eps = MLA_QK ** 0.5, MLA_QK * EPS
    tq = tables(qg_ref[...] * (root_n * MLA_QK ** -0.5 * LOG2E))
    kg = kg_ref[...] * root_n
    krr = rope(kr, tables(kg))
    kr_ss = _dot(jnp.square(kr).astype(BF16), ones_ref[:LANES, :LANES]) + n_eps
    heads = [(i, slice(i * LANES, (i + 1) * LANES)) for i in range(COL_BLOCK // LANES)]

    def q_pair(p, z):
        r = lax.rsqrt(head_sumsq(z) + n_eps)
        for i, sl in heads:
            q_ref[:, p * COL_BLOCK + i * LANES:p * COL_BLOCK + (i + 1) * LANES] = (rope(z[:, sl], tq) * r[:, sl]).astype(BF16)

    def k_pair(p, z):
        ss = head_sumsq(z)
        for i, sl in heads:
            rk = lax.rsqrt(ss[:, sl] + kr_ss)
            k_ref[:, p * COL_BLOCK + i * LANES:p * COL_BLOCK + (i + 1) * LANES] = (rk * (z[:, sl] * kg + krr)).astype(BF16)

    def v_pair(p, z):
        vt_ref[0, p * COL_BLOCK:(p + 1) * COL_BLOCK, :] = z.T.astype(BF16)

    n_kp = MLA_HEADS * LANES // COL_BLOCK
    n_vp = MLA_HEADS * MLA_V // COL_BLOCK
    items = []
    for p in range(n_kp):
        items += [(q_pair, p, cq, wuq_ref, p), (k_pair, p, ckvn, wukv_ref, p)]
        if p % (n_kp // n_vp) == 0:
            items.append((v_pair, p // (n_kp // n_vp), ckvn, wukv_ref, n_kp + p // (n_kp // n_vp)))
    _project_columns(items)


def _mla_attn_kernel(q_ref, k_ref, vt_ref, o_ref):
    problems = []
    for i in range(q_ref.shape[1] // LANES):
        sl = slice(i * LANES, (i + 1) * LANES)
        problems.append((k_ref[0, :, sl], q_ref[:, sl], vt_ref[0, i * MLA_V:(i + 1) * MLA_V, :]))
    outs = [ot * (1.0 / l) for ot, l in _attend_t(problems, MLA_LOOKAHEAD)]
    o_ref[...] = jnp.concatenate(outs, axis=0).T.astype(BF16)


def _l1_mix_ffn_kernel(*refs):
    h_refs, (o_ref, wo_ref, mod_ref, g_ref, wg_ref, wu_ref, wd_ref, out_ref) = refs[:FFN_SUBTILES], refs[FFN_SUBTILES:]
    m = mod_ref[0, 0]
    h = jnp.concatenate([h_ref[...] for h_ref in h_refs], axis=0)
    out_ref[...] = _ffn(h + m[2:3] * _dot(o_ref[...], wo_ref[...]), m, g_ref, wg_ref, wu_ref, wd_ref)


def _rope_tables(n_ctx, n_lat, rot_dim, first_lane):
    n_freq = rot_dim // 4
    half = rot_dim // 2
    freqs = ROPE_THETA ** (-jnp.arange(n_freq, dtype=F32) / n_freq)
    rows = n_lat // GRID_W
    row = jnp.repeat(jnp.arange(rows, dtype=F32), GRID_W)
    col = jnp.tile(jnp.arange(GRID_W, dtype=F32), rows)
    ang = jnp.concatenate([row[:, None] * freqs, col[:, None] * freqs], axis=-1)
    ang = jnp.concatenate([jnp.zeros((n_ctx, half), F32), ang], axis=0)
    cos_h, sin_h = jnp.cos(ang), jnp.sin(ang)
    zero = jnp.zeros_like(sin_h)
    n_rep = (LANES - first_lane) // rot_dim if first_lane == 0 else 1
    cos = jnp.concatenate([cos_h, cos_h] * n_rep, axis=-1)
    sa = jnp.concatenate([-sin_h, zero] * n_rep, axis=-1)
    sb = jnp.concatenate([zero, sin_h] * n_rep, axis=-1)
    n_rows = n_ctx + n_lat
    pad_lo = first_lane
    pad_hi = LANES - first_lane - cos.shape[1]
    cos = jnp.concatenate([jnp.ones((n_rows, pad_lo), F32), cos, jnp.ones((n_rows, pad_hi), F32)], axis=-1)
    sa = jnp.pad(sa, ((0, 0), (pad_lo, pad_hi)))
    sb = jnp.pad(sb, ((0, 0), (pad_lo, pad_hi)))
    return cos, sa, sb


def _pad_heads(w, width):
    k = w.shape[0]
    w = w.reshape(k, MLA_HEADS, width)
    return jnp.pad(w, ((0, 0), (0, 0), (0, LANES - width))).reshape(k, MLA_HEADS * LANES)


def kernel(x, c, ctx, c_ctx, mod_w, mod_b, norm_mix_g, norm_ffn_g, ffn_w_gate, ffn_w_up, ffn_w_down, ab_w_in, ab_w_out, pool_w, pool_scale, diff_q_norm_g, diff_k_norm_g, diff_lam_q1, diff_lam_k1, diff_lam_q2, diff_lam_k2, diff_subln_g, mla_w_dq, mla_q_lat_g, mla_w_uq, mla_w_dkv, mla_kv_lat_g, mla_w_ukv, mla_q_norm_g, mla_k_norm_g, mla_w_out):
    n_batch, n_lat, d = x.shape
    n_ctx = ctx.shape[1]
    n_tok = n_ctx + n_lat
    tm = ROW_TILE
    assert n_ctx == tm and n_lat % tm == 0 and n_lat >= 2 * tm and n_batch < MOD_ROWS
    assert mod_w.shape[0] == 2 and ab_w_in.shape[0] == 1 and mla_w_dq.shape[0] == 1
    tpb = n_tok // tm
    lpb = n_lat // tm
    n_rows = n_batch * n_tok
    ffn_hidden = ffn_w_gate.shape[-1]
    ctx_mod = n_batch

    cond = jnp.concatenate([c, c_ctx[None, :], jnp.zeros((MOD_ROWS - n_batch - 1, d), F32)], axis=0)
    mods = _adaln(cond, mod_w, mod_b)

    def lat_row(t):
        return (t // lpb) * tpb + 1 + t % lpb

    vec = lambda v: v.reshape(1, -1)
    tp = PROJ_SUBTILES * tm
    assert n_tok % tp == 0
    ppb = n_tok // tp
    pb_row = lambda w: pl.BlockSpec((tp, w), lambda b, j: (b * ppb + j, 0))
    pb_tab = pl.BlockSpec((tp, LANES), lambda b, j: (j, 0))
    pb_mod = lambda l: pl.BlockSpec((1, 1, 6, d), lambda b, j: (l, b, 0, 0))
    pb_cmod = lambda l: pl.BlockSpec((1, 1, 6, d), lambda b, j: (l, ctx_mod, 0, 0))
    tf = FFN_SUBTILES * tm
    assert n_lat % tf == 0
    bt_mod = lambda l: pl.BlockSpec((1, 1, 6, d), lambda b, j: (l, jnp.where(j == 0, ctx_mod, b), 0, 0))
    bt_row = lambda w: pl.BlockSpec((tm, w), lambda b, j: (b * tpb + j, 0))
    x_spec = pl.BlockSpec((1, tm, d), lambda b, j: (b, jnp.maximum(j - 1, 0), 0))
    ctx_spec = pl.BlockSpec((1, tm, d), lambda b, j: (b, 0, 0))
    ffn_w_specs = [_const_spec((1, d)), _const_spec((d, ffn_hidden)), _const_spec((d, ffn_hidden)),
                   _const_spec((ffn_hidden, d))]

    def ffn_weights(layer):
        return (vec(norm_ffn_g[layer]), ffn_w_gate[layer].astype(BF16), ffn_w_up[layer].astype(BF16),
                ffn_w_down[layer].astype(BF16))

    lam_init = 0.8 - 0.6 * math.exp(-0.3 * 0)
    cos, sa, sb = _rope_tables(n_ctx, n_lat, DIFF_HEAD_DIM, 0)
    grp = jnp.arange(COL_BLOCK) // DIFF_HEAD_DIM
    block_diag = (grp[:, None] == grp[None, :]).astype(BF16)
    qkv_shape = jax.ShapeDtypeStruct((n_rows, DIFF_WIDTH), BF16)
    u, q, k, vt = pl.pallas_call(
        _l0_proj_kernel,
        grid=(n_batch, ppb),
        in_specs=[pl.BlockSpec((1, tm, d), lambda b, j, s=s: (b, jnp.maximum(PROJ_SUBTILES * j + s - 1, 0), 0))
                  for s in range(PROJ_SUBTILES)]
                 + [ctx_spec, pb_mod(0), pb_cmod(0), _const_spec((1, d)), _const_spec(ab_w_in.shape[1:]),
                    _const_spec((1, LANES)), _const_spec((1, LANES)),
                    _const_spec((COL_BLOCK, COL_BLOCK)), pb_tab, pb_tab, pb_tab],
        out_specs=[pb_row(POOL_WIDTH), pb_row(DIFF_WIDTH), pb_row(DIFF_WIDTH),
                   pl.BlockSpec((1, DIFF_WIDTH, tp), lambda b, j: (b, 0, j))],
        out_shape=[jax.ShapeDtypeStruct((n_rows, POOL_WIDTH), BF16), qkv_shape, qkv_shape,
                   jax.ShapeDtypeStruct((n_batch, DIFF_WIDTH, n_tok), BF16)],
        compiler_params=_params("arbitrary", "arbitrary"),
        name="l0_proj",
    )(*([x] * PROJ_SUBTILES), ctx, mods, mods, vec(norm_mix_g[0]), ab_w_in[0].astype(BF16),
      vec(jnp.tile(diff_q_norm_g[0], LANES // DIFF_HEAD_DIM)), vec(jnp.tile(diff_k_norm_g[0], LANES // DIFF_HEAD_DIM)),
      block_diag, cos, sa, sb)

    dw = DIFF_HEADS_PER_STEP * DIFF_PAIR
    k_spec = pl.BlockSpec((1, n_tok, dw), lambda b, hd, j: (b, 0, hd))
    v_spec = pl.BlockSpec((1, dw, n_tok), lambda b, hd, j: (b, hd, 0))
    qo_spec = pl.BlockSpec((tm, dw), lambda b, hd, j: (b * tpb + j, hd))
    lam_spec = pl.BlockSpec((1, DIFF_HEAD_DIM), lambda b, hd, j: (0, 0))
    o = pl.pallas_call(
        functools.partial(_diff_attn_kernel, n_ctx=n_ctx, lam_init=lam_init),
        grid=(n_batch, DIFF_HEADS // DIFF_HEADS_PER_STEP, tpb),
        in_specs=[qo_spec, k_spec, v_spec, lam_spec, lam_spec, lam_spec, lam_spec,
                  pl.BlockSpec((1, DIFF_PAIR), lambda b, hd, j: (0, 0))],
        out_specs=qo_spec,
        out_shape=qkv_shape,
        compiler_params=_params("arbitrary", "arbitrary", "arbitrary"),
        name="diff_attn",
    )(q, k.reshape(n_batch, n_tok, DIFF_WIDTH), vt,
      vec(diff_lam_q1[0]), vec(diff_lam_k1[0]), vec(diff_lam_q2[0]), vec(diff_lam_k2[0]), vec(diff_subln_g[0]))

    h = pl.pallas_call(
        functools.partial(_l0_mix_ffn_kernel, n_ctx=n_ctx, n_tok=n_tok),
        grid=(n_batch, tpb),
        in_specs=[x_spec, ctx_spec, pl.BlockSpec((1, n_tok, POOL_WIDTH), lambda b, j: (b, 0, 0)), bt_row(DIFF_WIDTH),
                  _const_spec(pool_w.shape[1:]), _const_spec((1, POOL_WIDTH)), _const_spec(ab_w_out.shape[1:]),
                  bt_mod(0)] + ffn_w_specs,
        out_specs=bt_row(d),
        out_shape=jax.ShapeDtypeStruct((n_rows, d), F32),
        compiler_params=_params("arbitrary", "arbitrary"),
        name="l0_mix_ffn",
    )(x, ctx, u.reshape(n_batch, n_tok, POOL_WIDTH), o, pool_w[0].astype(BF16), vec(pool_scale[0]),
      ab_w_out[0].astype(BF16), mods, *ffn_weights(0))

    cos, sa, sb = _rope_tables(n_ctx, n_lat, MLA_ROPE, MLA_NOPE)
    w_uq = _pad_heads(mla_w_uq[0], MLA_QK).astype(BF16)
    w_dkv = mla_w_dkv[0]
    w_dkv = jnp.concatenate([
        w_dkv[:, :MLA_KV_RANK], jnp.zeros((d, MLA_NOPE), F32), w_dkv[:, MLA_KV_RANK:],
        jnp.zeros((d, LANES - MLA_QK), F32)], axis=-1).astype(BF16)
    w_ukv = mla_w_ukv[0].reshape(MLA_KV_RANK, MLA_HEADS, MLA_NOPE + MLA_V)
    w_ukv = jnp.concatenate([
        _pad_heads(w_ukv[:, :, :MLA_NOPE].reshape(MLA_KV_RANK, -1), MLA_NOPE),
        w_ukv[:, :, MLA_NOPE:].reshape(MLA_KV_RANK, -1)], axis=-1).astype(BF16)
    pad_gain = lambda g: vec(jnp.pad(g, (0, LANES - MLA_QK)))
    head_of = jnp.arange(2 * LANES) // LANES
    head_ones = (head_of[:, None] == head_of[None, :]).astype(BF16)
    hq = MLA_HEADS * LANES
    q, k, vt = pl.pallas_call(
        _l1_proj_kernel,
        grid=(n_batch, ppb),
        in_specs=[pb_row(d), pb_mod(1), pb_cmod(1), _const_spec((1, d)), _const_spec(mla_w_dq.shape[1:]),
                  _const_spec((1, mla_w_dq.shape[2])), _const_spec(w_uq.shape), _const_spec(w_dkv.shape),
                  _const_spec((1, MLA_KV_RANK)), _const_spec(w_ukv.shape), _const_spec((1, LANES)),
                  _const_spec((1, LANES)), pb_tab, pb_tab, pb_tab, _const_spec((2 * LANES, 2 * LANES))],
        out_specs=[pb_row(hq), pb_row(hq), pl.BlockSpec((1, MLA_HEADS * MLA_V, tp), lambda b, j: (b, 0, j))],
        out_shape=[jax.ShapeDtypeStruct((n_rows, hq), BF16), jax.ShapeDtypeStruct((n_rows, hq), BF16),
                   jax.ShapeDtypeStruct((n_batch, MLA_HEADS * MLA_V, n_tok), BF16)],
        compiler_params=_params("arbitrary", "arbitrary"),
        name="l1_proj",
    )(h, mods, mods, vec(norm_mix_g[1]), mla_w_dq[0].astype(BF16), vec(mla_q_lat_g[0]), w_uq, w_dkv,
      vec(mla_kv_lat_g[0]), w_ukv, pad_gain(mla_q_norm_g[0]), pad_gain(mla_k_norm_g[0]), cos, sa, sb,
      head_ones)

    n_lat_rows = n_batch * n_lat
    o = pl.pallas_call(
        _mla_attn_kernel,
        grid=(n_batch, MLA_HEADS // MLA_HEADS_PER_STEP, lpb),
        in_specs=[pl.BlockSpec((tm, MLA_HEADS_PER_STEP * LANES), lambda b, hp, j: (b * tpb + 1 + j, hp)),
                  pl.BlockSpec((1, n_tok, MLA_HEADS_PER_STEP * LANES), lambda b, hp, j: (b, 0, hp)),
                  pl.BlockSpec((1, MLA_HEADS_PER_STEP * MLA_V, n_tok), lambda b, hp, j: (b, hp, 0))],
        out_specs=pl.BlockSpec((tm, MLA_HEADS_PER_STEP * MLA_V), lambda b, hp, j: (b * lpb + j, hp)),
        out_shape=jax.ShapeDtypeStruct((n_lat_rows, MLA_HEADS * MLA_V), BF16),
        compiler_params=_params("arbitrary", "arbitrary", "arbitrary"),
        name="mla_attn",
    )(q, k.reshape(n_batch, n_tok, hq), vt)

    h = pl.pallas_call(
        _l1_mix_ffn_kernel,
        grid=(n_lat_rows // tf,),
        in_specs=[pl.BlockSpec((tm, d), lambda t, s=s: (lat_row(FFN_SUBTILES * t + s), 0)) for s in range(FFN_SUBTILES)]
                 + [pl.BlockSpec((tf, MLA_HEADS * MLA_V), lambda t: (t, 0)), _const_spec(mla_w_out.shape[1:]),
                    pl.BlockSpec((1, 1, 6, d), lambda t: (1, t * tf // n_lat, 0, 0))] + ffn_w_specs,
        out_specs=pl.BlockSpec((tf, d), lambda t: (t, 0)),
        out_shape=jax.ShapeDtypeStruct((n_lat_rows, d), F32),
        compiler_params=_params("arbitrary"),
        name="l1_mix_ffn",
    )(*([h] * FFN_SUBTILES), o, mla_w_out[0].astype(BF16), mods, *ffn_weights(1))
    return h.reshape(n_batch, n_lat, d)
```

```python
import functools
import math

import jax
import jax.numpy as jnp
from jax import lax
from jax.experimental import pallas as pl
from jax.experimental.pallas import tpu as pltpu

F32 = jnp.float32
BF16 = jnp.bfloat16

EPS = 1e-6
GRID_W = 64
ROPE_THETA = 10000.0
POOL_WINDOWS = (2, 4, 8, 16)
POOL_GROUP = 128
POOL_WIDTH = POOL_GROUP * len(POOL_WINDOWS)
DIFF_HEADS = 4
DIFF_HEAD_DIM = 64
DIFF_PAIR = 2 * DIFF_HEAD_DIM
DIFF_WIDTH = DIFF_HEADS * DIFF_PAIR
MLA_HEADS = 16
MLA_NOPE = 64
MLA_ROPE = 32
MLA_QK = MLA_NOPE + MLA_ROPE
MLA_V = 64
MLA_KV_RANK = 256
LANES = 128
ROW_TILE = 256
COL_BLOCK = 256
PROJ_SUBTILES = 3
FFN_SUBTILES = 2
MOD_ROWS = 16
ONES_ROWS = 16
LOG2E = math.log2(math.e)
DIFF_HEADS_PER_STEP = 4
MLA_HEADS_PER_STEP = 16
KEY_CHUNK = 256
PROJECT_LOOKAHEAD = 3
DIFF_LOOKAHEAD = 3
MLA_LOOKAHEAD = 1
VMEM_LIMIT = 56 * 1024 * 1024

_NT = (((1,), (1,)), ((), ()))


def _dot(a, b):
    return jnp.dot(a, b, preferred_element_type=F32)


def _rms(x, g):
    return x * lax.rsqrt(jnp.mean(x * x, axis=-1, keepdims=True) + EPS) * g


def _rms_mod(x, g, shift, scale):
    return x * lax.rsqrt(jnp.mean(x * x, axis=-1, keepdims=True) + EPS) * (g * (1.0 + scale)) + shift


def _params(*sem):
    return pltpu.CompilerParams(dimension_semantics=sem, vmem_limit_bytes=VMEM_LIMIT)


def _const_spec(shape):
    zeros = (0,) * len(shape)
    return pl.BlockSpec(shape, lambda *_: zeros, pipeline_mode=pl.Buffered(1))


def _adaln_kernel(cond_ref, w_ref, b_ref, o_ref):
    c = cond_ref[...]
    a = (c / (1.0 + jnp.exp(-c))).astype(BF16)
    o_ref[0, 0] = _dot(a, w_ref[0].astype(BF16)) + b_ref[0, 0]


def _adaln(cond, mod_w, mod_b):
    depth, d, d6 = mod_w.shape
    n = d6 // d
    out = pl.pallas_call(
        _adaln_kernel,
        grid=(depth, n),
        in_specs=[
            pl.BlockSpec((MOD_ROWS, d), lambda l, j: (0, 0)),
            pl.BlockSpec((1, d, d), lambda l, j: (l, 0, j)),
            pl.BlockSpec((1, 1, 1, d), lambda l, j: (l, j, 0, 0)),
        ],
        out_specs=pl.BlockSpec((1, 1, MOD_ROWS, d), lambda l, j: (l, j, 0, 0)),
        out_shape=jax.ShapeDtypeStruct((depth, n, MOD_ROWS, d), F32),
        compiler_params=_params("arbitrary", "arbitrary"),
        name="adaln",
    )(cond, mod_w, mod_b.reshape(depth, n, 1, d))
    return out.transpose(0, 2, 1, 3)


def _project_columns(items):
    project = lambda it: _dot(it[2], it[3][:, it[4] * COL_BLOCK:(it[4] + 1) * COL_BLOCK])
    zs = []
    for i, it in enumerate(items):
        while len(zs) < min(i + 1 + PROJECT_LOOKAHEAD, len(items)):
            zs.append(project(items[len(zs)]))
        it[0](it[1], zs[i])
        zs[i] = None


def _rope_fns(cos, sa, sb, half):
    def tables(g):
        gb = jnp.broadcast_to(g, cos.shape)
        return gb * cos, pltpu.roll(gb, LANES - half, 1) * sa, pltpu.roll(gb, half, 1) * sb

    def rope(z, t):
        return z * t[0] + pltpu.roll(z, LANES - half, 1) * t[1] + pltpu.roll(z, half, 1) * t[2]

    return tables, rope


def _l0_proj_kernel(*refs):
    x_refs, refs = refs[:PROJ_SUBTILES], refs[PROJ_SUBTILES:]
    (ctx_ref, mod_ref, cmod_ref, g_ref, w_ref, qg_ref, kg_ref, bd_ref, cos_ref, sa_ref, sb_ref,
     u_ref, q_ref, k_ref, vt_ref) = refs
    subs = []
    for s, x_ref in enumerate(x_refs):
        h, m = x_ref[0], mod_ref[0, 0]
        if s == 0:
            is_ctx = pl.program_id(1) == 0
            h, m = jnp.where(is_ctx, ctx_ref[0], h), jnp.where(is_ctx, cmod_ref[0, 0], m)
        subs.append(_rms_mod(h, g_ref[...], m[0:1], m[1:2]).astype(BF16))
    a = jnp.concatenate(subs, axis=0)
    tables, rope = _rope_fns(cos_ref[...], sa_ref[...], sb_ref[...], DIFF_HEAD_DIM // 2)
    root_n, n_eps = DIFF_HEAD_DIM ** 0.5, DIFF_HEAD_DIM * EPS
    tq = tables(qg_ref[...] * (root_n * DIFF_HEAD_DIM ** -0.5 * LOG2E))
    tk = tables(kg_ref[...] * root_n)

    def plain(ref):
        def finish(p, z):
            ref[:, p * COL_BLOCK:(p + 1) * COL_BLOCK] = z.astype(BF16)
        return finish

    def norm_rope(ref, t):
        def finish(p, z):
            r = lax.rsqrt(_dot(jnp.square(z).astype(BF16), bd_ref[...]) + n_eps)
            for i in range(COL_BLOCK // LANES):
                sl = slice(i * LANES, (i + 1) * LANES)
                ref[:, p * COL_BLOCK + i * LANES:p * COL_BLOCK + (i + 1) * LANES] = (rope(z[:, sl], t) * r[:, sl]).astype(BF16)
        return finish

    def transposed(p, z):
        vt_ref[0, p * COL_BLOCK:(p + 1) * COL_BLOCK, :] = z.T.astype(BF16)

    items, col = [], 0
    for finish, width in ((plain(u_ref), POOL_WIDTH), (norm_rope(q_ref, tq), DIFF_WIDTH),
                          (norm_rope(k_ref, tk), DIFF_WIDTH), (transposed, DIFF_WIDTH)):
        items += [(finish, p, a, w_ref, col + p) for p in range(width // COL_BLOCK)]
        col += width // COL_BLOCK
    _project_columns(items)


def _attend_t(problems, lookahead):
    def scores(p):
        return lax.dot_general(p[0], p[1], _NT, preferred_element_type=F32)

    def finish(st, vt):
        n, dv = st.shape[0], vt.shape[0]
        rs, ms = [], []
        for lo in range(0, n, KEY_CHUNK):
            hi = min(lo + KEY_CHUNK, n)
            sc = st[lo:hi]
            ms.append(jnp.max(sc, axis=0, keepdims=True))
            e = jnp.exp2(sc - ms[-1]).astype(BF16)
            lhs = jnp.concatenate([vt[:, lo:hi], jnp.ones((ONES_ROWS, hi - lo), BF16)], axis=0)
            rs.append(_dot(lhs, e))
        m_all = functools.reduce(jnp.maximum, ms)
        r = functools.reduce(jnp.add, [rc * jnp.exp2(mc - m_all) for rc, mc in zip(rs, ms)])
        return r[:dv], r[dv:dv + 1]

    outs, sts = [], []
    for i, p in enumerate(problems):
        while len(sts) < min(i + 1 + lookahead, len(problems)):
            sts.append(scores(problems[len(sts)]))
        outs.append(finish(sts[i], p[2]))
    return outs


def _diff_attn_kernel(q_ref, k_ref, vt_ref, lq1_ref, lk1_ref, lq2_ref, lk2_ref, sg_ref, o_ref, *, n_ctx, lam_init):
    j = pl.program_id(2)
    lam = (jnp.exp(jnp.sum(lq1_ref[...] * lk1_ref[...], axis=-1, keepdims=True))
           - jnp.exp(jnp.sum(lq2_ref[...] * lk2_ref[...], axis=-1, keepdims=True)) + lam_init)
    n_heads = q_ref.shape[1] // DIFF_PAIR
    lane = lax.broadcasted_iota(jnp.int32, (q_ref.shape[0], DIFF_PAIR), 1)

    def attend(n_keys):
        problems = []
        for hd in range(n_heads):
            sl = slice(hd * DIFF_PAIR, (hd + 1) * DIFF_PAIR)
            q = q_ref[:, sl].astype(F32)
            kk, vt = k_ref[0, :n_keys, sl], vt_ref[0, sl, :n_keys]
            problems.append((kk, jnp.where(lane < DIFF_HEAD_DIM, q, 0.0).astype(BF16), vt))
            problems.append((kk, jnp.where(lane >= DIFF_HEAD_DIM, q, 0.0).astype(BF16), vt))
        outs = _attend_t(problems, DIFF_LOOKAHEAD)
        for hd in range(n_heads):
            (o1, l1), (o2, l2) = outs[2 * hd], outs[2 * hd + 1]
            o = (o1 * (1.0 / l1) - o2 * (lam / l2)).T
            o_ref[:, hd * DIFF_PAIR:(hd + 1) * DIFF_PAIR] = (_rms(o, sg_ref[...]) * (1.0 - lam_init)).astype(BF16)

    @pl.when(j == 0)
    def _():
        attend(n_ctx)

    @pl.when(j > 0)
    def _():
        attend(k_ref.shape[1])


def _ffn(hs, ms, g_ref, wg_ref, wu_ref, wd_ref):
    tm = hs[0].shape[0]
    a = jnp.concatenate([_rms_mod(h, g_ref[...], m[3:4], m[4:5]).astype(BF16) for h, m in zip(hs, ms)], axis=0)
    project = lambda c: (_dot(a, wg_ref[:, c * COL_BLOCK:(c + 1) * COL_BLOCK]),
                         _dot(a, wu_ref[:, c * COL_BLOCK:(c + 1) * COL_BLOCK]))
    n_chunks = wg_ref.shape[1] // COL_BLOCK
    hid, nxt = [], project(0)
    for c in range(n_chunks):
        (gate, up), nxt = nxt, (project(c + 1) if c + 1 < n_chunks else None)
        hid.append((gate / (1.0 + jnp.exp(-gate)) * up).astype(BF16))
    down = _dot(jnp.concatenate(hid, axis=-1), wd_ref[...])
    return [h + m[5:6] * down[s * tm:(s + 1) * tm] for s, (h, m) in enumerate(zip(hs, ms))]


def _l0_mix_ffn_kernel(*refs, n_ctx, n_tok):
    x_refs, refs = refs[:PROJ_SUBTILES], refs[PROJ_SUBTILES:]
    ctx_ref, u_ref, o_ref, pw_ref, ps_ref, wo_ref, mod_ref, cmod_ref, g_ref, wg_ref, wu_ref, wd_ref, out_ref = refs
    tm = ROW_TILE
    win = 2 * tm
    ps = ps_ref[...]
    groups = [slice(g * POOL_GROUP, (g + 1) * POOL_GROUP) for g in range(len(POOL_WINDOWS))]
    attn_part = _dot(o_ref[...], wo_ref[POOL_WIDTH:, :])
    hs, ms, uts, means = [], [], [], []
    for s, x_ref in enumerate(x_refs):
        r = PROJ_SUBTILES * pl.program_id(1) + s
        h, m, seg_lo, seg_hi = x_ref[0], mod_ref[0, 0], n_ctx, n_tok
        ws = jnp.clip(r * tm - tm // 2, n_ctx, n_tok - win)
        if s == 0:
            is_ctx = r == 0
            h, m = jnp.where(is_ctx, ctx_ref[0], h), jnp.where(is_ctx, cmod_ref[0, 0], m)
            seg_lo, seg_hi, ws = jnp.where(is_ctx, 0, n_ctx), jnp.where(is_ctx, n_ctx, n_tok), jnp.where(is_ctx, 0, ws)
        hs.append(h)
        ms.append(m)
        uw = u_ref[0, pl.ds(pl.multiple_of(ws, LANES), win), :]
        uts.append(u_ref[0, pl.ds(pl.multiple_of(r * tm, tm), tm), :].astype(F32))
        row = r * tm + lax.broadcasted_iota(jnp.int32, (tm, 1), 0)
        col = ws + lax.broadcasted_iota(jnp.int32, (tm, win), 1)
        for sl, w in zip(groups, POOL_WINDOWS):
            lo = jnp.maximum(row - w // 2, seg_lo)
            hi = jnp.minimum(row - w // 2 + w, seg_hi)
            band = jnp.where(col >= lo, jnp.where(col < hi, 1.0, 0.0), 0.0).astype(BF16)
            means.append(_dot(band, uw[:, sl]) / (hi - lo).astype(F32))
    n_g = len(groups)
    ys = []
    for g, sl in enumerate(groups):
        dev = jnp.concatenate([means[s * n_g + g] - uts[s][:, sl] for s in range(len(hs))], axis=0)
        ys.append((_dot(dev.astype(BF16), pw_ref[g]) * ps[:, sl]).astype(BF16))
    mixed = _dot(jnp.concatenate(ys, axis=-1), wo_ref[:POOL_WIDTH, :]) + attn_part
    h1 = [h + m[2:3] * mixed[s * tm:(s + 1) * tm] for s, (h, m) in enumerate(zip(hs, ms))]
    for s, out in enumerate(_ffn(h1, ms, g_ref, wg_ref, wu_ref, wd_ref)):
        out_ref[s * tm:(s + 1) * tm, :] = out


def _l1_proj_kernel(h_ref, mod_ref, cmod_ref, g_ref, wdq_ref, qlg_ref, wuq_ref, wdkv_ref, kvg_ref, wukv_ref,
                    qg_ref, kg_ref, cos_ref, sa_ref, sb_ref, ones_ref, q_ref, k_ref, vt_ref):
    subs = []
    for s in range(h_ref.shape[0] // ROW_TILE):
        is_ctx = (pl.program_id(1) == 0) if s == 0 else False
        m = jnp.where(is_ctx, cmod_ref[0, 0], mod_ref[0, 0]) if s == 0 else mod_ref[0, 0]
        subs.append(_rms_mod(h_ref[s * ROW_TILE:(s + 1) * ROW_TILE, :], g_ref[...], m[0:1], m[1:2]).astype(BF16))
    a = jnp.concatenate(subs, axis=0)
    tables, rope = _rope_fns(cos_ref[...], sa_ref[...], sb_ref[...], MLA_ROPE // 2)

    def head_sumsq(z):
        return _dot(jnp.square(z).astype(BF16), ones_ref[...])

    cq = _rms(_dot(a, wdq_ref[...]), qlg_ref[...]).astype(BF16)
    ckv = _dot(a, wdkv_ref[...])
    kr = ckv[:, MLA_KV_RANK:]
    ckvn = _rms(ckv[:, :MLA_KV_RANK], kvg_ref[...]).astype(BF16)
    root_n, n_eps = MLA_QK ** 0.5, MLA_QK * EPS
    tq = tables(qg_ref[...] * (root_n * MLA_QK ** -0.5 * LOG2E))
    kg = kg_ref[...] * root_n
    krr = rope(kr, tables(kg))
    kr_ss = _dot(jnp.square(kr).astype(BF16), ones_ref[:LANES, :LANES]) + n_eps
    heads = [(i, slice(i * LANES, (i + 1) * LANES)) for i in range(COL_BLOCK // LANES)]

    def q_pair(p, z):
        r = lax.rsqrt(head_sumsq(z) + n_eps)
        for i, sl in heads:
            q_ref[:, p * COL_BLOCK + i * LANES:p * COL_BLOCK + (i + 1) * LANES] = (rope(z[:, sl], tq) * r[:, sl]).astype(BF16)

    def k_pair(p, z):
        ss = head_sumsq(z)
        for i, sl in heads:
            rk = lax.rsqrt(ss[:, sl] + kr_ss)
            k_ref[:, p * COL_BLOCK + i * LANES:p * COL_BLOCK + (i + 1) * LANES] = (rk * (z[:, sl] * kg + krr)).astype(BF16)

    def v_pair(p, z):
        vt_ref[0, p * COL_BLOCK:(p + 1) * COL_BLOCK, :] = z.T.astype(BF16)

    n_kp = MLA_HEADS * LANES // COL_BLOCK
    n_vp = MLA_HEADS * MLA_V // COL_BLOCK
    items = []
    for p in range(n_kp):
        items += [(q_pair, p, cq, wuq_ref, p), (k_pair, p, ckvn, wukv_ref, p)]
        if p % (n_kp // n_vp) == 0:
            items.append((v_pair, p // (n_kp // n_vp), ckvn, wukv_ref, n_kp + p // (n_kp // n_vp)))
    _project_columns(items)


def _mla_attn_kernel(q_ref, k_ref, vt_ref, o_ref):
    problems = []
    for i in range(q_ref.shape[1] // LANES):
        sl = slice(i * LANES, (i + 1) * LANES)
        problems.append((k_ref[0, :, sl], q_ref[:, sl], vt_ref[0, i * MLA_V:(i + 1) * MLA_V, :]))
    outs = [ot * (1.0 / l) for ot, l in _attend_t(problems, MLA_LOOKAHEAD)]
    o_ref[...] = jnp.concatenate(outs, axis=0).T.astype(BF16)


def _l1_mix_ffn_kernel(*refs):
    h_refs, (o_ref, wo_ref, mod_ref, g_ref, wg_ref, wu_ref, wd_ref, out_ref) = refs[:FFN_SUBTILES], refs[FFN_SUBTILES:]
    m = mod_ref[0, 0]
    tm = h_refs[0].shape[0]
    mixed = _dot(o_ref[...], wo_ref[...])
    h1 = [h_ref[...] + m[2:3] * mixed[s * tm:(s + 1) * tm] for s, h_ref in enumerate(h_refs)]
    for s, out in enumerate(_ffn(h1, [m] * len(h1), g_ref, wg_ref, wu_ref, wd_ref)):
        out_ref[s * tm:(s + 1) * tm, :] = out


def _rope_tables(n_ctx, n_lat, rot_dim, first_lane):
    n_freq = rot_dim // 4
    half = rot_dim // 2
    freqs = ROPE_THETA ** (-jnp.arange(n_freq, dtype=F32) / n_freq)
    rows = n_lat // GRID_W
    row = jnp.repeat(jnp.arange(rows, dtype=F32), GRID_W)
    col = jnp.tile(jnp.arange(GRID_W, dtype=F32), rows)
    ang = jnp.concatenate([row[:, None] * freqs, col[:, None] * freqs], axis=-1)
    ang = jnp.concatenate([jnp.zeros((n_ctx, half), F32), ang], axis=0)
    cos_h, sin_h = jnp.cos(ang), jnp.sin(ang)
    zero = jnp.zeros_like(sin_h)
    n_rep = (LANES - first_lane) // rot_dim if first_lane == 0 else 1
    cos = jnp.concatenate([cos_h, cos_h] * n_rep, axis=-1)
    sa = jnp.concatenate([-sin_h, zero] * n_rep, axis=-1)
    sb = jnp.concatenate([zero, sin_h] * n_rep, axis=-1)
    n_rows = n_ctx + n_lat
    pad_lo = first_lane
    pad_hi = LANES - first_lane - cos.shape[1]
    cos = jnp.concatenate([jnp.ones((n_rows, pad_lo), F32), cos, jnp.ones((n_rows, pad_hi), F32)], axis=-1)
    sa = jnp.pad(sa, ((0, 0), (pad_lo, pad_hi)))
    sb = jnp.pad(sb, ((0, 0), (pad_lo, pad_hi)))
    return cos, sa, sb


def _pad_heads(w, width):
    k = w.shape[0]
    w = w.reshape(k, MLA_HEADS, width)
    return jnp.pad(w, ((0, 0), (0, 0), (0, LANES - width))).reshape(k, MLA_HEADS * LANES)


def kernel(x, c, ctx, c_ctx, mod_w, mod_b, norm_mix_g, norm_ffn_g, ffn_w_gate, ffn_w_up, ffn_w_down, ab_w_in, ab_w_out, pool_w, pool_scale, diff_q_norm_g, diff_k_norm_g, diff_lam_q1, diff_lam_k1, diff_lam_q2, diff_lam_k2, diff_subln_g, mla_w_dq, mla_q_lat_g, mla_w_uq, mla_w_dkv, mla_kv_lat_g, mla_w_ukv, mla_q_norm_g, mla_k_norm_g, mla_w_out):
    n_batch, n_lat, d = x.shape
    n_ctx = ctx.shape[1]
    n_tok = n_ctx + n_lat
    tm = ROW_TILE
    assert n_ctx == tm and n_lat % tm == 0 and n_lat >= 2 * tm and n_batch < MOD_ROWS
    assert mod_w.shape[0] == 2 and ab_w_in.shape[0] == 1 and mla_w_dq.shape[0] == 1
    tpb = n_tok // tm
    lpb = n_lat // tm
    n_rows = n_batch * n_tok
    ffn_hidden = ffn_w_gate.shape[-1]
    ctx_mod = n_batch

    cond = jnp.concatenate([c, c_ctx[None, :], jnp.zeros((MOD_ROWS - n_batch - 1, d), F32)], axis=0)
    mods = _adaln(cond, mod_w, mod_b)

    def lat_row(t):
        return (t // lpb) * tpb + 1 + t % lpb

    vec = lambda v: v.reshape(1, -1)
    tp = PROJ_SUBTILES * tm
    assert n_tok % tp == 0
    ppb = n_tok // tp
    pb_row = lambda w: pl.BlockSpec((tp, w), lambda b, j: (b * ppb + j, 0))
    pb_tab = pl.BlockSpec((tp, LANES), lambda b, j: (j, 0))
    pb_mod = lambda l: pl.BlockSpec((1, 1, 6, d), lambda b, j: (l, b, 0, 0))
    pb_cmod = lambda l: pl.BlockSpec((1, 1, 6, d), lambda b, j: (l, ctx_mod, 0, 0))
    tf = FFN_SUBTILES * tm
    assert n_lat % tf == 0
    x_specs = [pl.BlockSpec((1, tm, d), lambda b, j, s=s: (b, jnp.maximum(PROJ_SUBTILES * j + s - 1, 0), 0))
               for s in range(PROJ_SUBTILES)]
    ctx_spec = pl.BlockSpec((1, tm, d), lambda b, j: (b, 0, 0))
    ffn_w_specs = [_const_spec((1, d)), _const_spec((d, ffn_hidden)), _const_spec((d, ffn_hidden)),
                   _const_spec((ffn_hidden, d))]

    def ffn_weights(layer):
        return (vec(norm_ffn_g[layer]), ffn_w_gate[layer].astype(BF16), ffn_w_up[layer].astype(BF16),
                ffn_w_down[layer].astype(BF16))

    lam_init = 0.8 - 0.6 * math.exp(-0.3 * 0)
    cos, sa, sb = _rope_tables(n_ctx, n_lat, DIFF_HEAD_DIM, 0)
    grp = jnp.arange(COL_BLOCK) // DIFF_HEAD_DIM
    block_diag = (grp[:, None] == grp[None, :]).astype(BF16)
    qkv_shape = jax.ShapeDtypeStruct((n_rows, DIFF_WIDTH), BF16)
    u, q, k, vt = pl.pallas_call(
        _l0_proj_kernel,
        grid=(n_batch, ppb),
        in_specs=x_specs + [ctx_spec, pb_mod(0), pb_cmod(0), _const_spec((1, d)), _const_spec(ab_w_in.shape[1:]),
                    _const_spec((1, LANES)), _const_spec((1, LANES)),
                    _const_spec((COL_BLOCK, COL_BLOCK)), pb_tab, pb_tab, pb_tab],
        out_specs=[pb_row(POOL_WIDTH), pb_row(DIFF_WIDTH), pb_row(DIFF_WIDTH),
                   pl.BlockSpec((1, DIFF_WIDTH, tp), lambda b, j: (b, 0, j))],
        out_shape=[jax.ShapeDtypeStruct((n_rows, POOL_WIDTH), BF16), qkv_shape, qkv_shape,
                   jax.ShapeDtypeStruct((n_batch, DIFF_WIDTH, n_tok), BF16)],
        compiler_params=_params("arbitrary", "arbitrary"),
        name="l0_proj",
    )(*([x] * PROJ_SUBTILES), ctx, mods, mods, vec(norm_mix_g[0]), ab_w_in[0].astype(BF16),
      vec(jnp.tile(diff_q_norm_g[0], LANES // DIFF_HEAD_DIM)), vec(jnp.tile(diff_k_norm_g[0], LANES // DIFF_HEAD_DIM)),
      block_diag, cos, sa, sb)

    dw = DIFF_HEADS_PER_STEP * DIFF_PAIR
    k_spec = pl.BlockSpec((1, n_tok, dw), lambda b, hd, j: (b, 0, hd))
    v_spec = pl.BlockSpec((1, dw, n_tok), lambda b, hd, j: (b, hd, 0))
    qo_spec = pl.BlockSpec((tm, dw), lambda b, hd, j: (b * tpb + j, hd))
    lam_spec = pl.BlockSpec((1, DIFF_HEAD_DIM), lambda b, hd, j: (0, 0))
    o = pl.pallas_call(
        functools.partial(_diff_attn_kernel, n_ctx=n_ctx, lam_init=lam_init),
        grid=(n_batch, DIFF_HEADS // DIFF_HEADS_PER_STEP, tpb),
        in_specs=[qo_spec, k_spec, v_spec, lam_spec, lam_spec, lam_spec, lam_spec,
                  pl.BlockSpec((1, DIFF_PAIR), lambda b, hd, j: (0, 0))],
        out_specs=qo_spec,
        out_shape=qkv_shape,
        compiler_params=_params("arbitrary", "arbitrary", "arbitrary"),
        name="diff_attn",
    )(q, k.reshape(n_batch, n_tok, DIFF_WIDTH), vt,
      vec(diff_lam_q1[0]), vec(diff_lam_k1[0]), vec(diff_lam_q2[0]), vec(diff_lam_k2[0]), vec(diff_subln_g[0]))

    h = pl.pallas_call(
        functools.partial(_l0_mix_ffn_kernel, n_ctx=n_ctx, n_tok=n_tok),
        grid=(n_batch, ppb),
        in_specs=x_specs + [ctx_spec, pl.BlockSpec((1, n_tok, POOL_WIDTH), lambda b, j: (b, 0, 0)), pb_row(DIFF_WIDTH),
                            _const_spec(pool_w.shape[1:]), _const_spec((1, POOL_WIDTH)),
                            _const_spec(ab_w_out.shape[1:]), pb_mod(0), pb_cmod(0)] + ffn_w_specs,
        out_specs=pb_row(d),
        out_shape=jax.ShapeDtypeStruct((n_rows, d), F32),
        compiler_params=_params("arbitrary", "arbitrary"),
        name="l0_mix_ffn",
    )(*([x] * PROJ_SUBTILES), ctx, u.reshape(n_batch, n_tok, POOL_WIDTH), o, pool_w[0].astype(BF16),
      vec(pool_scale[0]), ab_w_out[0].astype(BF16), mods, mods, *ffn_weights(0))

    cos, sa, sb = _rope_tables(n_ctx, n_lat, MLA_ROPE, MLA_NOPE)
    w_uq = _pad_heads(mla_w_uq[0], MLA_QK).astype(BF16)
    w_dkv = mla_w_dkv[0]
    w_dkv = jnp.concatenate([
        w_dkv[:, :MLA_KV_RANK], jnp.zeros((d, MLA_NOPE), F32), w_dkv[:, MLA_KV_RANK:],
        jnp.zeros((d, LANES - MLA_QK), F32)], axis=-1).astype(BF16)
    w_ukv = mla_w_ukv[0].reshape(MLA_KV_RANK, MLA_HEADS, MLA_NOPE + MLA_V)
    w_ukv = jnp.concatenate([
        _pad_heads(w_ukv[:, :, :MLA_NOPE].reshape(MLA_KV_RANK, -1), MLA_NOPE),
        w_ukv[:, :, MLA_NOPE:].reshape(MLA_KV_RANK, -1)], axis=-1).astype(BF16)
    pad_gain = lambda g: vec(jnp.pad(g, (0, LANES - MLA_QK)))
    head_of = jnp.arange(2 * LANES) // LANES
    head_ones = (head_of[:, None] == head_of[None, :]).astype(BF16)
    hq = MLA_HEADS * LANES
    q, k, vt = pl.pallas_call(
        _l1_proj_kernel,
        grid=(n_batch, ppb),
        in_specs=[pb_row(d), pb_mod(1), pb_cmod(1), _const_spec((1, d)), _const_spec(mla_w_dq.shape[1:]),
                  _const_spec((1, mla_w_dq.shape[2])), _const_spec(w_uq.shape), _const_spec(w_dkv.shape),
                  _const_spec((1, MLA_KV_RANK)), _const_spec(w_ukv.shape), _const_spec((1, LANES)),
                  _const_spec((1, LANES)), pb_tab, pb_tab, pb_tab, _const_spec((2 * LANES, 2 * LANES))],
        out_specs=[pb_row(hq), pb_row(hq), pl.BlockSpec((1, MLA_HEADS * MLA_V, tp), lambda b, j: (b, 0, j))],
        out_shape=[jax.ShapeDtypeStruct((n_rows, hq), BF16), jax.ShapeDtypeStruct((n_rows, hq), BF16),
                   jax.ShapeDtypeStruct((n_batch, MLA_HEADS * MLA_V, n_tok), BF16)],
        compiler_params=_params("arbitrary", "arbitrary"),
        name="l1_proj",
    )(h, mods, mods, vec(norm_mix_g[1]), mla_w_dq[0].astype(BF16), vec(mla_q_lat_g[0]), w_uq, w_dkv,
      vec(mla_kv_lat_g[0]), w_ukv, pad_gain(mla_q_norm_g[0]), pad_gain(mla_k_norm_g[0]), cos, sa, sb,
      head_ones)

    n_lat_rows = n_batch * n_lat
    o = pl.pallas_call(
        _mla_attn_kernel,
        grid=(n_batch, MLA_HEADS // MLA_HEADS_PER_STEP, lpb),
        in_specs=[pl.BlockSpec((tm, MLA_HEADS_PER_STEP * LANES), lambda b, hp, j: (b * tpb + 1 + j, hp)),
                  pl.BlockSpec((1, n_tok, MLA_HEADS_PER_STEP * LANES), lambda b, hp, j: (b, 0, hp)),
                  pl.BlockSpec((1, MLA_HEADS_PER_STEP * MLA_V, n_tok), lambda b, hp, j: (b, hp, 0))],
        out_specs=pl.BlockSpec((tm, MLA_HEADS_PER_STEP * MLA_V), lambda b, hp, j: (b * lpb + j, hp)),
        out_shape=jax.ShapeDtypeStruct((n_lat_rows, MLA_HEADS * MLA_V), BF16),
        compiler_params=_params("arbitrary", "arbitrary", "arbitrary"),
        name="mla_attn",
    )(q, k.reshape(n_batch, n_tok, hq), vt)

    h = pl.pallas_call(
        _l1_mix_ffn_kernel,
        grid=(n_lat_rows // tf,),
        in_specs=[pl.BlockSpec((tm, d), lambda t, s=s: (lat_row(FFN_SUBTILES * t + s), 0)) for s in range(FFN_SUBTILES)]
                 + [pl.BlockSpec((tf, MLA_HEADS * MLA_V), lambda t: (t, 0)), _const_spec(mla_w_out.shape[1:]),
                    pl.BlockSpec((1, 1, 6, d), lambda t: (1, t * tf // n_lat, 0, 0))] + ffn_w_specs,
        out_specs=pl.BlockSpec((tf, d), lambda t: (t, 0)),
        out_shape=jax.ShapeDtypeStruct((n_lat_rows, d), F32),
        compiler_params=_params("arbitrary"),
        name="l1_mix_ffn",
    )(*([h] * FFN_SUBTILES), o, mla_w_out[0].astype(BF16), mods, *ffn_weights(1))
    return h.reshape(n_batch, n_lat, d)
```

```python
import functools
import math

import jax
import jax.numpy as jnp
from jax import lax
from jax.experimental import pallas as pl
from jax.experimental.pallas import tpu as pltpu

F32 = jnp.float32
BF16 = jnp.bfloat16

EPS = 1e-6
GRID_W = 64
ROPE_THETA = 10000.0
POOL_WINDOWS = (2, 4, 8, 16)
POOL_GROUP = 128
POOL_WIDTH = POOL_GROUP * len(POOL_WINDOWS)
DIFF_HEADS = 4
DIFF_HEAD_DIM = 64
DIFF_PAIR = 2 * DIFF_HEAD_DIM
DIFF_WIDTH = DIFF_HEADS * DIFF_PAIR
MLA_HEADS = 16
MLA_NOPE = 64
MLA_ROPE = 32
MLA_QK = MLA_NOPE + MLA_ROPE
MLA_V = 64
MLA_KV_RANK = 256
LANES = 128
ROW_TILE = 256
COL_BLOCK = 256
PROJ_SUBTILES = 3
FFN_SUBTILES = 2
MOD_ROWS = 16
ONES_ROWS = 16
LOG2E = math.log2(math.e)
DIFF_HEADS_PER_STEP = 4
MLA_HEADS_PER_STEP = 16
MLA_QTILES_PER_STEP = 2
DIFF_QTILES_PER_STEP = 2
KEY_CHUNK = 256
PROJECT_LOOKAHEAD = 3
DIFF_LOOKAHEAD = 3
MLA_LOOKAHEAD = 1
VMEM_LIMIT = 56 * 1024 * 1024

_NT = (((1,), (1,)), ((), ()))


def _dot(a, b):
    return jnp.dot(a, b, preferred_element_type=F32)


def _rms(x, g):
    return x * lax.rsqrt(jnp.mean(x * x, axis=-1, keepdims=True) + EPS) * g


def _rms_mod(x, g, shift, scale):
    return x * lax.rsqrt(jnp.mean(x * x, axis=-1, keepdims=True) + EPS) * (g * (1.0 + scale)) + shift


def _params(*sem):
    return pltpu.CompilerParams(dimension_semantics=sem, vmem_limit_bytes=VMEM_LIMIT)


def _const_spec(shape):
    zeros = (0,) * len(shape)
    return pl.BlockSpec(shape, lambda *_: zeros, pipeline_mode=pl.Buffered(1))


def _adaln_kernel(cond_ref, w_ref, b_ref, o_ref):
    c = cond_ref[...]
    a = (c / (1.0 + jnp.exp(-c))).astype(BF16)
    o_ref[0, 0] = _dot(a, w_ref[0].astype(BF16)) + b_ref[0, 0]


def _adaln(cond, mod_w, mod_b):
    depth, d, d6 = mod_w.shape
    n = d6 // d
    out = pl.pallas_call(
        _adaln_kernel,
        grid=(depth, n),
        in_specs=[
            pl.BlockSpec((MOD_ROWS, d), lambda l, j: (0, 0)),
            pl.BlockSpec((1, d, d), lambda l, j: (l, 0, j)),
            pl.BlockSpec((1, 1, 1, d), lambda l, j: (l, j, 0, 0)),
        ],
        out_specs=pl.BlockSpec((1, 1, MOD_ROWS, d), lambda l, j: (l, j, 0, 0)),
        out_shape=jax.ShapeDtypeStruct((depth, n, MOD_ROWS, d), F32),
        compiler_params=_params("arbitrary", "arbitrary"),
        name="adaln",
    )(cond, mod_w, mod_b.reshape(depth, n, 1, d))
    return out.transpose(0, 2, 1, 3)


def _project_columns(items):
    project = lambda it: _dot(it[2], it[3][:, it[4] * COL_BLOCK:(it[4] + 1) * COL_BLOCK])
    zs = []
    for i, it in enumerate(items):
        while len(zs) < min(i + 1 + PROJECT_LOOKAHEAD, len(items)):
            zs.append(project(items[len(zs)]))
        it[0](it[1], zs[i])
        zs[i] = None


def _rope_fns(cos, sa, sb, half):
    def tables(g):
        gb = jnp.broadcast_to(g, cos.shape)
        return gb * cos, pltpu.roll(gb, LANES - half, 1) * sa, pltpu.roll(gb, half, 1) * sb

    def rope(z, t):
        return z * t[0] + pltpu.roll(z, LANES - half, 1) * t[1] + pltpu.roll(z, half, 1) * t[2]

    return tables, rope


def _l0_proj_kernel(*refs):
    x_refs, refs = refs[:PROJ_SUBTILES], refs[PROJ_SUBTILES:]
    (ctx_ref, mod_ref, cmod_ref, g_ref, w_ref, qg_ref, kg_ref, bd_ref, cos_ref, sa_ref, sb_ref,
     u_ref, q_ref, k_ref, vt_ref) = refs
    subs = []
    for s, x_ref in enumerate(x_refs):
        h, m = x_ref[0], mod_ref[0, 0]
        if s == 0:
            is_ctx = pl.program_id(1) == 0
            h, m = jnp.where(is_ctx, ctx_ref[0], h), jnp.where(is_ctx, cmod_ref[0, 0], m)
        subs.append(_rms_mod(h, g_ref[...], m[0:1], m[1:2]).astype(BF16))
    a = jnp.concatenate(subs, axis=0)
    tables, rope = _rope_fns(cos_ref[...], sa_ref[...], sb_ref[...], DIFF_HEAD_DIM // 2)
    root_n, n_eps = DIFF_HEAD_DIM ** 0.5, DIFF_HEAD_DIM * EPS
    tq = tables(qg_ref[...] * (root_n * DIFF_HEAD_DIM ** -0.5 * LOG2E))
    tk = tables(kg_ref[...] * root_n)

    def plain(ref):
        def finish(p, z):
            ref[:, p * COL_BLOCK:(p + 1) * COL_BLOCK] = z.astype(BF16)
        return finish

    def norm_rope(ref, t):
        def finish(p, z):
            r = lax.rsqrt(_dot(jnp.square(z).astype(BF16), bd_ref[...]) + n_eps)
            for i in range(COL_BLOCK // LANES):
                sl = slice(i * LANES, (i + 1) * LANES)
                ref[:, p * COL_BLOCK + i * LANES:p * COL_BLOCK + (i + 1) * LANES] = (rope(z[:, sl], t) * r[:, sl]).astype(BF16)
        return finish

    def transposed(p, z):
        vt_ref[0, p * COL_BLOCK:(p + 1) * COL_BLOCK, :] = z.T.astype(BF16)

    items, col = [], 0
    for finish, width in ((plain(u_ref), POOL_WIDTH), (norm_rope(q_ref, tq), DIFF_WIDTH),
                          (norm_rope(k_ref, tk), DIFF_WIDTH), (transposed, DIFF_WIDTH)):
        items += [(finish, p, a, w_ref, col + p) for p in range(width // COL_BLOCK)]
        col += width // COL_BLOCK
    _project_columns(items)


def _attend_t(problems, lookahead):
    def scores(p):
        return lax.dot_general(p[0], p[1], _NT, preferred_element_type=F32)

    def finish(st, vt):
        n, dv = st.shape[0], vt.shape[0]
        rs, ms = [], []
        for lo in range(0, n, KEY_CHUNK):
            hi = min(lo + KEY_CHUNK, n)
            sc = st[lo:hi]
            ms.append(jnp.max(sc, axis=0, keepdims=True))
            e = jnp.exp2(sc - ms[-1]).astype(BF16)
            lhs = jnp.concatenate([vt[:, lo:hi], jnp.ones((ONES_ROWS, hi - lo), BF16)], axis=0)
            rs.append(_dot(lhs, e))
        m_all = functools.reduce(jnp.maximum, ms)
        r = functools.reduce(jnp.add, [rc * jnp.exp2(mc - m_all) for rc, mc in zip(rs, ms)])
        return r[:dv], r[dv:dv + 1]

    outs, sts = [], []
    for i, p in enumerate(problems):
        while len(sts) < min(i + 1 + lookahead, len(problems)):
            sts.append(scores(problems[len(sts)]))
        outs.append(finish(sts[i], p[2]))
    return outs


def _diff_attn_kernel(*refs, n_ctx, lam_init):
    q_refs, refs = refs[:DIFF_QTILES_PER_STEP], refs[DIFF_QTILES_PER_STEP:]
    k_ref, vt_ref, lq1_ref, lk1_ref, lq2_ref, lk2_ref, sg_ref, o_ctx_ref, o_lat_ref = refs
    j = pl.program_id(2)
    lam = (jnp.exp(jnp.sum(lq1_ref[...] * lk1_ref[...], axis=-1, keepdims=True))
           - jnp.exp(jnp.sum(lq2_ref[...] * lk2_ref[...], axis=-1, keepdims=True)) + lam_init)
    tq = q_refs[0].shape[0]
    n_heads = q_refs[0].shape[1] // DIFF_PAIR
    lane = lax.broadcasted_iota(jnp.int32, (tq, DIFF_PAIR), 1)

    def attend(tiles, n_keys, o_ref):
        problems = []
        for q_ref in tiles:
            for hd in range(n_heads):
                sl = slice(hd * DIFF_PAIR, (hd + 1) * DIFF_PAIR)
                q = q_ref[:, sl].astype(F32)
                kk, vt = k_ref[0, :n_keys, sl], vt_ref[0, sl, :n_keys]
                problems.append((kk, jnp.where(lane < DIFF_HEAD_DIM, q, 0.0).astype(BF16), vt))
                problems.append((kk, jnp.where(lane >= DIFF_HEAD_DIM, q, 0.0).astype(BF16), vt))
        outs = _attend_t(problems, DIFF_LOOKAHEAD)
        for t in range(len(tiles)):
            for hd in range(n_heads):
                (o1, l1), (o2, l2) = outs[2 * (t * n_heads + hd)], outs[2 * (t * n_heads + hd) + 1]
                o = (o1 * (1.0 / l1) - o2 * (lam / l2)).T
                o_ref[t * tq:(t + 1) * tq, hd * DIFF_PAIR:(hd + 1) * DIFF_PAIR] = (
                    _rms(o, sg_ref[...]) * (1.0 - lam_init)).astype(BF16)

    @pl.when(j == 0)
    def _():
        attend(q_refs[:1], n_ctx, o_ctx_ref)

    @pl.when(j > 0)
    def _():
        attend(q_refs, k_ref.shape[1], o_lat_ref)


def _ffn(hs, ms, g_ref, wg_ref, wu_ref, wd_ref):
    tm = hs[0].shape[0]
    a = jnp.concatenate([_rms_mod(h, g_ref[...], m[3:4], m[4:5]).astype(BF16) for h, m in zip(hs, ms)], axis=0)
    project = lambda c: (_dot(a, wg_ref[:, c * COL_BLOCK:(c + 1) * COL_BLOCK]),
                         _dot(a, wu_ref[:, c * COL_BLOCK:(c + 1) * COL_BLOCK]))
    n_chunks = wg_ref.shape[1] // COL_BLOCK
    hid, nxt = [], project(0)
    for c in range(n_chunks):
        (gate, up), nxt = nxt, (project(c + 1) if c + 1 < n_chunks else None)
        hid.append((gate / (1.0 + jnp.exp(-gate)) * up).astype(BF16))
    down = _dot(jnp.concatenate(hid, axis=-1), wd_ref[...])
    return [h + m[5:6] * down[s * tm:(s + 1) * tm] for s, (h, m) in enumerate(zip(hs, ms))]


def _l0_mix_ffn_kernel(*refs, n_ctx, n_tok):
    x_refs, o_refs, refs = refs[:PROJ_SUBTILES], refs[PROJ_SUBTILES:2 * PROJ_SUBTILES], refs[2 * PROJ_SUBTILES:]
    ctx_ref, o_ctx_ref, u_ref, pw_ref, ps_ref, wo_ref, mod_ref, cmod_ref, g_ref, wg_ref, wu_ref, wd_ref, out_ref = refs
    tm = ROW_TILE
    win = 2 * tm
    ps = ps_ref[...]
    groups = [slice(g * POOL_GROUP, (g + 1) * POOL_GROUP) for g in range(len(POOL_WINDOWS))]
    first_is_ctx = pl.program_id(1) == 0
    o_all = jnp.concatenate([jnp.where(first_is_ctx, o_ctx_ref[...], o_refs[0][...])]
                            + [o_ref[...] for o_ref in o_refs[1:]], axis=0)
    attn_part = _dot(o_all, wo_ref[POOL_WIDTH:, :])
    hs, ms, uts, means = [], [], [], []
    for s, x_ref in enumerate(x_refs):
        r = PROJ_SUBTILES * pl.program_id(1) + s
        h, m, seg_lo, seg_hi = x_ref[0], mod_ref[0, 0], n_ctx, n_tok
        ws = jnp.clip(r * tm - tm // 2, n_ctx, n_tok - win)
        if s == 0:
            is_ctx = r == 0
            h, m = jnp.where(is_ctx, ctx_ref[0], h), jnp.where(is_ctx, cmod_ref[0, 0], m)
            seg_lo, seg_hi, ws = jnp.where(is_ctx, 0, n_ctx), jnp.where(is_ctx, n_ctx, n_tok), jnp.where(is_ctx, 0, ws)
        hs.append(h)
        ms.append(m)
        uw = u_ref[0, pl.ds(pl.multiple_of(ws, LANES), win), :]
        uts.append(u_ref[0, pl.ds(pl.multiple_of(r * tm, tm), tm), :].astype(F32))
        row = r * tm + lax.broadcasted_iota(jnp.int32, (tm, 1), 0)
        col = ws + lax.broadcasted_iota(jnp.int32, (tm, win), 1)
        for sl, w in zip(groups, POOL_WINDOWS):
            lo = jnp.maximum(row - w // 2, seg_lo)
            hi = jnp.minimum(row - w // 2 + w, seg_hi)
            band = jnp.where(col >= lo, jnp.where(col < hi, 1.0, 0.0), 0.0).astype(BF16)
            means.append(_dot(band, uw[:, sl]) / (hi - lo).astype(F32))
    n_g = len(groups)
    ys = []
    for g, sl in enumerate(groups):
        dev = jnp.concatenate([means[s * n_g + g] - uts[s][:, sl] for s in range(len(hs))], axis=0)
        ys.append((_dot(dev.astype(BF16), pw_ref[g]) * ps[:, sl]).astype(BF16))
    mixed = _dot(jnp.concatenate(ys, axis=-1), wo_ref[:POOL_WIDTH, :]) + attn_part
    h1 = [h + m[2:3] * mixed[s * tm:(s + 1) * tm] for s, (h, m) in enumerate(zip(hs, ms))]
    for s, out in enumerate(_ffn(h1, ms, g_ref, wg_ref, wu_ref, wd_ref)):
        out_ref[s * tm:(s + 1) * tm, :] = out


def _l1_proj_kernel(h_ref, mod_ref, cmod_ref, g_ref, wdq_ref, qlg_ref, wuq_ref, wdkv_ref, kvg_ref, wukv_ref,
                    qg_ref, kg_ref, cos_ref, sa_ref, sb_ref, ones_ref, q_ref, k_ref, vt_ref):
    subs = []
    for s in range(h_ref.shape[0] // ROW_TILE):
        is_ctx = (pl.program_id(1) == 0) if s == 0 else False
        m = jnp.where(is_ctx, cmod_ref[0, 0], mod_ref[0, 0]) if s == 0 else mod_ref[0, 0]
        subs.append(_rms_mod(h_ref[s * ROW_TILE:(s + 1) * ROW_TILE, :], g_ref[...], m[0:1], m[1:2]).astype(BF16))
    a = jnp.concatenate(subs, axis=0)
    tables, rope = _rope_fns(cos_ref[...], sa_ref[...], sb_ref[...], MLA_ROPE // 2)

    def head_sumsq(z):
        return _dot(jnp.square(z).astype(BF16), ones_ref[...])

    cq = _rms(_dot(a, wdq_ref[...]), qlg_ref[...]).astype(BF16)
    ckv = _dot(a, wdkv_ref[...])
    kr = ckv[:, MLA_KV_RANK:]
    ckvn = _rms(ckv[:, :MLA_KV_RANK], kvg_ref[...]).astype(BF16)
    root_n, n_eps = MLA_QK ** 0.5, MLA_QK * EPS
    tq = tables(qg_ref[...] * (root_n * MLA_QK ** -0.5 * LOG2E))
    kg = kg_ref[...] * root_n
    krr = rope(kr, tables(kg))
    kr_ss = _dot(jnp.square(kr).astype(BF16), ones_ref[:LANES, :LANES]) + n_eps
    heads = [(i, slice(i * LANES, (i + 1) * LANES)) for i in range(COL_BLOCK // LANES)]

    def q_pair(p, z):
        r = lax.rsqrt(head_sumsq(z) + n_eps)
        for i, sl in heads:
            q_ref[:, p * COL_BLOCK + i * LANES:p * COL_BLOCK + (i + 1) * LANES] = (rope(z[:, sl], tq) * r[:, sl]).astype(BF16)

    def k_pair(p, z):
        ss = head_sumsq(z)
        for i, sl in heads:
            rk = lax.rsqrt(ss[:, sl] + kr_ss)
            k_ref[:, p * COL_BLOCK + i * LANES:p * COL_BLOCK + (i + 1) * LANES] = (rk * (z[:, sl] * kg + krr)).astype(BF16)

    def v_pair(p, z):
        vt_ref[0, p * COL_BLOCK:(p + 1) * COL_BLOCK, :] = z.T.astype(BF16)

    n_kp = MLA_HEADS * LANES // COL_BLOCK
    n_vp = MLA_HEADS * MLA_V // COL_BLOCK
    items = []
    for p in range(n_kp):
        items += [(q_pair, p, cq, wuq_ref, p), (k_pair, p, ckvn, wukv_ref, p)]
        if p % (n_kp // n_vp) == 0:
            items.append((v_pair, p // (n_kp // n_vp), ckvn, wukv_ref, n_kp + p // (n_kp // n_vp)))
    _project_columns(items)


def _mla_attn_kernel(*refs):
    q_refs, (k_ref, vt_ref, o_ref) = refs[:MLA_QTILES_PER_STEP], refs[MLA_QTILES_PER_STEP:]
    n_heads, tq = k_ref.shape[2] // LANES, q_refs[0].shape[0]
    problems = []
    for q_ref in q_refs:
        for i in range(n_heads):
            sl = slice(i * LANES, (i + 1) * LANES)
            problems.append((k_ref[0, :, sl], q_ref[:, sl], vt_ref[0, i * MLA_V:(i + 1) * MLA_V, :]))
    outs = [ot * (1.0 / l) for ot, l in _attend_t(problems, MLA_LOOKAHEAD)]
    for t in range(len(q_refs)):
        o_ref[t * tq:(t + 1) * tq, :] = jnp.concatenate(outs[t * n_heads:(t + 1) * n_heads], axis=0).T.astype(BF16)


def _l1_mix_ffn_kernel(*refs):
    h_refs, (o_ref, wo_ref, mod_ref, g_ref, wg_ref, wu_ref, wd_ref, out_ref) = refs[:FFN_SUBTILES], refs[FFN_SUBTILES:]
    m = mod_ref[0, 0]
    tm = h_refs[0].shape[0]
    mixed = _dot(o_ref[...], wo_ref[...])
    h1 = [h_ref[...] + m[2:3] * mixed[s * tm:(s + 1) * tm] for s, h_ref in enumerate(h_refs)]
    for s, out in enumerate(_ffn(h1, [m] * len(h1), g_ref, wg_ref, wu_ref, wd_ref)):
        out_ref[s * tm:(s + 1) * tm, :] = out


def _rope_tables(n_ctx, n_lat, rot_dim, first_lane):
    n_freq = rot_dim // 4
    half = rot_dim // 2
    freqs = ROPE_THETA ** (-jnp.arange(n_freq, dtype=F32) / n_freq)
    rows = n_lat // GRID_W
    row = jnp.repeat(jnp.arange(rows, dtype=F32), GRID_W)
    col = jnp.tile(jnp.arange(GRID_W, dtype=F32), rows)
    ang = jnp.concatenate([row[:, None] * freqs, col[:, None] * freqs], axis=-1)
    ang = jnp.concatenate([jnp.zeros((n_ctx, half), F32), ang], axis=0)
    cos_h, sin_h = jnp.cos(ang), jnp.sin(ang)
    zero = jnp.zeros_like(sin_h)
    n_rep = (LANES - first_lane) // rot_dim if first_lane == 0 else 1
    cos = jnp.concatenate([cos_h, cos_h] * n_rep, axis=-1)
    sa = jnp.concatenate([-sin_h, zero] * n_rep, axis=-1)
    sb = jnp.concatenate([zero, sin_h] * n_rep, axis=-1)
    n_rows = n_ctx + n_lat
    pad_lo = first_lane
    pad_hi = LANES - first_lane - cos.shape[1]
    cos = jnp.concatenate([jnp.ones((n_rows, pad_lo), F32), cos, jnp.ones((n_rows, pad_hi), F32)], axis=-1)
    sa = jnp.pad(sa, ((0, 0), (pad_lo, pad_hi)))
    sb = jnp.pad(sb, ((0, 0), (pad_lo, pad_hi)))
    return cos, sa, sb


def _pad_heads(w, width):
    k = w.shape[0]
    w = w.reshape(k, MLA_HEADS, width)
    return jnp.pad(w, ((0, 0), (0, 0), (0, LANES - width))).reshape(k, MLA_HEADS * LANES)


def kernel(x, c, ctx, c_ctx, mod_w, mod_b, norm_mix_g, norm_ffn_g, ffn_w_gate, ffn_w_up, ffn_w_down, ab_w_in, ab_w_out, pool_w, pool_scale, diff_q_norm_g, diff_k_norm_g, diff_lam_q1, diff_lam_k1, diff_lam_q2, diff_lam_k2, diff_subln_g, mla_w_dq, mla_q_lat_g, mla_w_uq, mla_w_dkv, mla_kv_lat_g, mla_w_ukv, mla_q_norm_g, mla_k_norm_g, mla_w_out):
    n_batch, n_lat, d = x.shape
    n_ctx = ctx.shape[1]
    n_tok = n_ctx + n_lat
    tm = ROW_TILE
    assert n_ctx == tm and n_lat % tm == 0 and n_lat >= 2 * tm and n_batch < MOD_ROWS
    assert mod_w.shape[0] == 2 and ab_w_in.shape[0] == 1 and mla_w_dq.shape[0] == 1
    tpb = n_tok // tm
    lpb = n_lat // tm
    n_rows = n_batch * n_tok
    ffn_hidden = ffn_w_gate.shape[-1]
    ctx_mod = n_batch

    cond = jnp.concatenate([c, c_ctx[None, :], jnp.zeros((MOD_ROWS - n_batch - 1, d), F32)], axis=0)
    mods = _adaln(cond, mod_w, mod_b)

    def lat_row(t):
        return (t // lpb) * tpb + 1 + t % lpb

    vec = lambda v: v.reshape(1, -1)
    tp = PROJ_SUBTILES * tm
    assert n_tok % tp == 0
    ppb = n_tok // tp
    pb_row = lambda w: pl.BlockSpec((tp, w), lambda b, j: (b * ppb + j, 0))
    pb_tab = pl.BlockSpec((tp, LANES), lambda b, j: (j, 0))
    pb_mod = lambda l: pl.BlockSpec((1, 1, 6, d), lambda b, j: (l, b, 0, 0))
    pb_cmod = lambda l: pl.BlockSpec((1, 1, 6, d), lambda b, j: (l, ctx_mod, 0, 0))
    tf = FFN_SUBTILES * tm
    assert n_lat % tf == 0
    x_specs = [pl.BlockSpec((1, tm, d), lambda b, j, s=s: (b, jnp.maximum(PROJ_SUBTILES * j + s - 1, 0), 0))
               for s in range(PROJ_SUBTILES)]
    ctx_spec = pl.BlockSpec((1, tm, d), lambda b, j: (b, 0, 0))
    o_specs = [pl.BlockSpec((tm, DIFF_WIDTH), lambda b, j, s=s: (b * lpb + jnp.maximum(PROJ_SUBTILES * j + s - 1, 0), 0))
               for s in range(PROJ_SUBTILES)]
    ffn_w_specs = [_const_spec((1, d)), _const_spec((d, ffn_hidden)), _const_spec((d, ffn_hidden)),
                   _const_spec((ffn_hidden, d))]

    def ffn_weights(layer):
        return (vec(norm_ffn_g[layer]), ffn_w_gate[layer].astype(BF16), ffn_w_up[layer].astype(BF16),
                ffn_w_down[layer].astype(BF16))

    lam_init = 0.8 - 0.6 * math.exp(-0.3 * 0)
    cos, sa, sb = _rope_tables(n_ctx, n_lat, DIFF_HEAD_DIM, 0)
    grp = jnp.arange(COL_BLOCK) // DIFF_HEAD_DIM
    block_diag = (grp[:, None] == grp[None, :]).astype(BF16)
    qkv_shape = jax.ShapeDtypeStruct((n_rows, DIFF_WIDTH), BF16)
    u, q, k, vt = pl.pallas_call(
        _l0_proj_kernel,
        grid=(n_batch, ppb),
        in_specs=x_specs + [ctx_spec, pb_mod(0), pb_cmod(0), _const_spec((1, d)), _const_spec(ab_w_in.shape[1:]),
                    _const_spec((1, LANES)), _const_spec((1, LANES)),
                    _const_spec((COL_BLOCK, COL_BLOCK)), pb_tab, pb_tab, pb_tab],
        out_specs=[pb_row(POOL_WIDTH), pb_row(DIFF_WIDTH), pb_row(DIFF_WIDTH),
                   pl.BlockSpec((1, DIFF_WIDTH, tp), lambda b, j: (b, 0, j))],
        out_shape=[jax.ShapeDtypeStruct((n_rows, POOL_WIDTH), BF16), qkv_shape, qkv_shape,
                   jax.ShapeDtypeStruct((n_batch, DIFF_WIDTH, n_tok), BF16)],
        compiler_params=_params("arbitrary", "arbitrary"),
        name="l0_proj",
    )(*([x] * PROJ_SUBTILES), ctx, mods, mods, vec(norm_mix_g[0]), ab_w_in[0].astype(BF16),
      vec(jnp.tile(diff_q_norm_g[0], LANES // DIFF_HEAD_DIM)), vec(jnp.tile(diff_k_norm_g[0], LANES // DIFF_HEAD_DIM)),
      block_diag, cos, sa, sb)

    dw = DIFF_HEADS_PER_STEP * DIFF_PAIR
    k_spec = pl.BlockSpec((1, n_tok, dw), lambda b, hd, j: (b, 0, hd))
    v_spec = pl.BlockSpec((1, dw, n_tok), lambda b, hd, j: (b, hd, 0))
    lam_spec = pl.BlockSpec((1, DIFF_HEAD_DIM), lambda b, hd, j: (0, 0))
    dq = DIFF_QTILES_PER_STEP
    assert lpb % dq == 0
    q_specs = [pl.BlockSpec((tm, dw), lambda b, hd, j, t=t: (b * tpb + jnp.maximum(dq * (j - 1) + 1, 0) + t, hd))
               for t in range(dq)]
    o_ctx, o_lat = pl.pallas_call(
        functools.partial(_diff_attn_kernel, n_ctx=n_ctx, lam_init=lam_init),
        grid=(n_batch, DIFF_HEADS // DIFF_HEADS_PER_STEP, 1 + lpb // dq),
        in_specs=q_specs + [k_spec, v_spec, lam_spec, lam_spec, lam_spec, lam_spec,
                            pl.BlockSpec((1, DIFF_PAIR), lambda b, hd, j: (0, 0))],
        out_specs=[pl.BlockSpec((tm, dw), lambda b, hd, j: (b, hd)),
                   pl.BlockSpec((dq * tm, dw), lambda b, hd, j: (b * (lpb // dq) + jnp.maximum(j - 1, 0), hd))],
        out_shape=[jax.ShapeDtypeStruct((n_batch * n_ctx, DIFF_WIDTH), BF16),
                   jax.ShapeDtypeStruct((n_batch * n_lat, DIFF_WIDTH), BF16)],
        compiler_params=_params("arbitrary", "arbitrary", "arbitrary"),
        name="diff_attn",
    )(*([q] * dq), k.reshape(n_batch, n_tok, DIFF_WIDTH), vt,
      vec(diff_lam_q1[0]), vec(diff_lam_k1[0]), vec(diff_lam_q2[0]), vec(diff_lam_k2[0]), vec(diff_subln_g[0]))

    h = pl.pallas_call(
        functools.partial(_l0_mix_ffn_kernel, n_ctx=n_ctx, n_tok=n_tok),
        grid=(n_batch, ppb),
        in_specs=x_specs + o_specs + [ctx_spec, pl.BlockSpec((tm, DIFF_WIDTH), lambda b, j: (b, 0)),
                                      pl.BlockSpec((1, n_tok, POOL_WIDTH), lambda b, j: (b, 0, 0)),
                            _const_spec(pool_w.shape[1:]), _const_spec((1, POOL_WIDTH)),
                            _const_spec(ab_w_out.shape[1:]), pb_mod(0), pb_cmod(0)] + ffn_w_specs,
        out_specs=pb_row(d),
        out_shape=jax.ShapeDtypeStruct((n_rows, d), F32),
        compiler_params=_params("arbitrary", "arbitrary"),
        name="l0_mix_ffn",
    )(*([x] * PROJ_SUBTILES), *([o_lat] * PROJ_SUBTILES), ctx, o_ctx, u.reshape(n_batch, n_tok, POOL_WIDTH),
      pool_w[0].astype(BF16),
      vec(pool_scale[0]), ab_w_out[0].astype(BF16), mods, mods, *ffn_weights(0))

    cos, sa, sb = _rope_tables(n_ctx, n_lat, MLA_ROPE, MLA_NOPE)
    w_uq = _pad_heads(mla_w_uq[0], MLA_QK).astype(BF16)
    w_dkv = mla_w_dkv[0]
    w_dkv = jnp.concatenate([
        w_dkv[:, :MLA_KV_RANK], jnp.zeros((d, MLA_NOPE), F32), w_dkv[:, MLA_KV_RANK:],
        jnp.zeros((d, LANES - MLA_QK), F32)], axis=-1).astype(BF16)
    w_ukv = mla_w_ukv[0].reshape(MLA_KV_RANK, MLA_HEADS, MLA_NOPE + MLA_V)
    w_ukv = jnp.concatenate([
        _pad_heads(w_ukv[:, :, :MLA_NOPE].reshape(MLA_KV_RANK, -1), MLA_NOPE),
        w_ukv[:, :, MLA_NOPE:].reshape(MLA_KV_RANK, -1)], axis=-1).astype(BF16)
    pad_gain = lambda g: vec(jnp.pad(g, (0, LANES - MLA_QK)))
    head_of = jnp.arange(2 * LANES) // LANES
    head_ones = (head_of[:, None] == head_of[None, :]).astype(BF16)
    hq = MLA_HEADS * LANES
    q, k, vt = pl.pallas_call(
        _l1_proj_kernel,
        grid=(n_batch, ppb),
        in_specs=[pb_row(d), pb_mod(1), pb_cmod(1), _const_spec((1, d)), _const_spec(mla_w_dq.shape[1:]),
                  _const_spec((1, mla_w_dq.shape[2])), _const_spec(w_uq.shape), _const_spec(w_dkv.shape),
                  _const_spec((1, MLA_KV_RANK)), _const_spec(w_ukv.shape), _const_spec((1, LANES)),
                  _const_spec((1, LANES)), pb_tab, pb_tab, pb_tab, _const_spec((2 * LANES, 2 * LANES))],
        out_specs=[pb_row(hq), pb_row(hq), pl.BlockSpec((1, MLA_HEADS * MLA_V, tp), lambda b, j: (b, 0, j))],
        out_shape=[jax.ShapeDtypeStruct((n_rows, hq), BF16), jax.ShapeDtypeStruct((n_rows, hq), BF16),
                   jax.ShapeDtypeStruct((n_batch, MLA_HEADS * MLA_V, n_tok), BF16)],
        compiler_params=_params("arbitrary", "arbitrary"),
        name="l1_proj",
    )(h, mods, mods, vec(norm_mix_g[1]), mla_w_dq[0].astype(BF16), vec(mla_q_lat_g[0]), w_uq, w_dkv,
      vec(mla_kv_lat_g[0]), w_ukv, pad_gain(mla_q_norm_g[0]), pad_gain(mla_k_norm_g[0]), cos, sa, sb,
      head_ones)

    n_lat_rows = n_batch * n_lat
    qpt = MLA_QTILES_PER_STEP
    assert lpb % qpt == 0
    o = pl.pallas_call(
        _mla_attn_kernel,
        grid=(n_batch, MLA_HEADS // MLA_HEADS_PER_STEP, lpb // qpt),
        in_specs=[pl.BlockSpec((tm, MLA_HEADS_PER_STEP * LANES), lambda b, hp, j, t=t: (b * tpb + 1 + qpt * j + t, hp))
                  for t in range(qpt)]
                 + [pl.BlockSpec((1, n_tok, MLA_HEADS_PER_STEP * LANES), lambda b, hp, j: (b, 0, hp)),
                    pl.BlockSpec((1, MLA_HEADS_PER_STEP * MLA_V, n_tok), lambda b, hp, j: (b, hp, 0))],
        out_specs=pl.BlockSpec((qpt * tm, MLA_HEADS_PER_STEP * MLA_V), lambda b, hp, j: (b * (lpb // qpt) + j, hp)),
        out_shape=jax.ShapeDtypeStruct((n_lat_rows, MLA_HEADS * MLA_V), BF16),
        compiler_params=_params("arbitrary", "arbitrary", "arbitrary"),
        name="mla_attn",
    )(*([q] * qpt), k.reshape(n_batch, n_tok, hq), vt)

    h = pl.pallas_call(
        _l1_mix_ffn_kernel,
        grid=(n_lat_rows // tf,),
        in_specs=[pl.BlockSpec((tm, d), lambda t, s=s: (lat_row(FFN_SUBTILES * t + s), 0)) for s in range(FFN_SUBTILES)]
                 + [pl.BlockSpec((tf, MLA_HEADS * MLA_V), lambda t: (t, 0)), _const_spec(mla_w_out.shape[1:]),
                    pl.BlockSpec((1, 1, 6, d), lambda t: (1, t * tf // n_lat, 0, 0))] + ffn_w_specs,
        out_specs=pl.BlockSpec((tf, d), lambda t: (t, 0)),
        out_shape=jax.ShapeDtypeStruct((n_lat_rows, d), F32),
        compiler_params=_params("arbitrary"),
        name="l1_mix_ffn",
    )(*([h] * FFN_SUBTILES), o, mla_w_out[0].astype(BF16), mods, *ffn_weights(1))
    return h.reshape(n_batch, n_lat, d)
```

```python
import functools
import math

import jax
import jax.numpy as jnp
from jax import lax
from jax.experimental import pallas as pl
from jax.experimental.pallas import tpu as pltpu

F32 = jnp.float32
BF16 = jnp.bfloat16

EPS = 1e-6
GRID_W = 64
ROPE_THETA = 10000.0
POOL_WINDOWS = (2, 4, 8, 16)
POOL_GROUP = 128
POOL_WIDTH = POOL_GROUP * len(POOL_WINDOWS)
DIFF_HEADS = 4
DIFF_HEAD_DIM = 64
DIFF_PAIR = 2 * DIFF_HEAD_DIM
DIFF_WIDTH = DIFF_HEADS * DIFF_PAIR
MLA_HEADS = 16
MLA_NOPE = 64
MLA_ROPE = 32
MLA_QK = MLA_NOPE + MLA_ROPE
MLA_V = 64
MLA_KV_RANK = 256
LANES = 128
ROW_TILE = 256
COL_BLOCK = 256
PROJ_SUBTILES = 3
FFN_SUBTILES = 4
MOD_ROWS = 16
ONES_ROWS = 16
LOG2E = math.log2(math.e)
DIFF_HEADS_PER_STEP = 4
MLA_HEADS_PER_STEP = 16
MLA_QTILES_PER_STEP = 2
DIFF_QTILES_PER_STEP = 2
KEY_CHUNK = 256
PROJECT_LOOKAHEAD = 3
DIFF_LOOKAHEAD = 3
MLA_LOOKAHEAD = 1
VMEM_LIMIT = 56 * 1024 * 1024

_NT = (((1,), (1,)), ((), ()))


def _dot(a, b):
    return jnp.dot(a, b, preferred_element_type=F32)


def _rms(x, g):
    return x * lax.rsqrt(jnp.mean(x * x, axis=-1, keepdims=True) + EPS) * g


def _rms_mod(x, g, shift, scale):
    return x * lax.rsqrt(jnp.mean(x * x, axis=-1, keepdims=True) + EPS) * (g * (1.0 + scale)) + shift


def _params(*sem):
    return pltpu.CompilerParams(dimension_semantics=sem, vmem_limit_bytes=VMEM_LIMIT)


def _const_spec(shape):
    zeros = (0,) * len(shape)
    return pl.BlockSpec(shape, lambda *_: zeros, pipeline_mode=pl.Buffered(1))


def _adaln_kernel(cond_ref, w_ref, b_ref, o_ref):
    c = cond_ref[...]
    a = (c / (1.0 + jnp.exp(-c))).astype(BF16)
    o_ref[0, 0] = _dot(a, w_ref[0].astype(BF16)) + b_ref[0, 0]


def _adaln(cond, mod_w, mod_b):
    depth, d, d6 = mod_w.shape
    n = d6 // d
    out = pl.pallas_call(
        _adaln_kernel,
        grid=(depth, n),
        in_specs=[
            pl.BlockSpec((MOD_ROWS, d), lambda l, j: (0, 0)),
            pl.BlockSpec((1, d, d), lambda l, j: (l, 0, j)),
            pl.BlockSpec((1, 1, 1, d), lambda l, j: (l, j, 0, 0)),
        ],
        out_specs=pl.BlockSpec((1, 1, MOD_ROWS, d), lambda l, j: (l, j, 0, 0)),
        out_shape=jax.ShapeDtypeStruct((depth, n, MOD_ROWS, d), F32),
        compiler_params=_params("arbitrary", "arbitrary"),
        name="adaln",
    )(cond, mod_w, mod_b.reshape(depth, n, 1, d))
    return out.transpose(0, 2, 1, 3)


def _project_columns(items):
    project = lambda it: _dot(it[2], it[3][:, it[4] * COL_BLOCK:(it[4] + 1) * COL_BLOCK])
    zs = []
    for i, it in enumerate(items):
        while len(zs) < min(i + 1 + PROJECT_LOOKAHEAD, len(items)):
            zs.append(project(items[len(zs)]))
        it[0](it[1], zs[i])
        zs[i] = None


def _rope_fns(cos, sa, sb, half):
    def tables(g):
        gb = jnp.broadcast_to(g, cos.shape)
        return gb * cos, pltpu.roll(gb, LANES - half, 1) * sa, pltpu.roll(gb, half, 1) * sb

    def rope(z, t):
        return z * t[0] + pltpu.roll(z, LANES - half, 1) * t[1] + pltpu.roll(z, half, 1) * t[2]

    return tables, rope


def _l0_proj_kernel(*refs):
    x_refs, refs = refs[:PROJ_SUBTILES], refs[PROJ_SUBTILES:]
    (ctx_ref, mod_ref, cmod_ref, g_ref, w_ref, qg_ref, kg_ref, bd_ref, cos_ref, sa_ref, sb_ref,
     u_ref, q_ref, k_ref, vt_ref) = refs
    subs = []
    for s, x_ref in enumerate(x_refs):
        h, m = x_ref[0], mod_ref[0, 0]
        if s == 0:
            is_ctx = pl.program_id(1) == 0
            h, m = jnp.where(is_ctx, ctx_ref[0], h), jnp.where(is_ctx, cmod_ref[0, 0], m)
        subs.append(_rms_mod(h, g_ref[...], m[0:1], m[1:2]).astype(BF16))
    a = jnp.concatenate(subs, axis=0)
    tables, rope = _rope_fns(cos_ref[...], sa_ref[...], sb_ref[...], DIFF_HEAD_DIM // 2)
    root_n, n_eps = DIFF_HEAD_DIM ** 0.5, DIFF_HEAD_DIM * EPS
    tq = tables(qg_ref[...] * (root_n * DIFF_HEAD_DIM ** -0.5 * LOG2E))
    tk = tables(kg_ref[...] * root_n)

    def plain(ref):
        def finish(p, z):
            ref[:, p * COL_BLOCK:(p + 1) * COL_BLOCK] = z.astype(BF16)
        return finish

    def norm_rope(ref, t):
        def finish(p, z):
            r = lax.rsqrt(_dot(jnp.square(z).astype(BF16), bd_ref[...]) + n_eps)
            for i in range(COL_BLOCK // LANES):
                sl = slice(i * LANES, (i + 1) * LANES)
                ref[:, p * COL_BLOCK + i * LANES:p * COL_BLOCK + (i + 1) * LANES] = (rope(z[:, sl], t) * r[:, sl]).astype(BF16)
        return finish

    def transposed(p, z):
        vt_ref[0, p * COL_BLOCK:(p + 1) * COL_BLOCK, :] = z.T.astype(BF16)

    items, col = [], 0
    for finish, width in ((plain(u_ref), POOL_WIDTH), (norm_rope(q_ref, tq), DIFF_WIDTH),
                          (norm_rope(k_ref, tk), DIFF_WIDTH), (transposed, DIFF_WIDTH)):
        items += [(finish, p, a, w_ref, col + p) for p in range(width // COL_BLOCK)]
        col += width // COL_BLOCK
    _project_columns(items)


def _attend_t(problems, lookahead):
    def scores(p):
        return lax.dot_general(p[0], p[1], _NT, preferred_element_type=F32)

    def finish(st, vt):
        n, dv = st.shape[0], vt.shape[0]
        rs, ms = [], []
        for lo in range(0, n, KEY_CHUNK):
            hi = min(lo + KEY_CHUNK, n)
            sc = st[lo:hi]
            ms.append(jnp.max(sc, axis=0, keepdims=True))
            e = jnp.exp2(sc - ms[-1]).astype(BF16)
            lhs = jnp.concatenate([vt[:, lo:hi], jnp.ones((ONES_ROWS, hi - lo), BF16)], axis=0)
            rs.append(_dot(lhs, e))
        m_all = functools.reduce(jnp.maximum, ms)
        r = functools.reduce(jnp.add, [rc * jnp.exp2(mc - m_all) for rc, mc in zip(rs, ms)])
        return r[:dv], r[dv:dv + 1]

    outs, sts = [], []
    for i, p in enumerate(problems):
        while len(sts) < min(i + 1 + lookahead, len(problems)):
            sts.append(scores(problems[len(sts)]))
        outs.append(finish(sts[i], p[2]))
    return outs


def _diff_attn_kernel(*refs, n_ctx, lam_init):
    q_refs, refs = refs[:DIFF_QTILES_PER_STEP], refs[DIFF_QTILES_PER_STEP:]
    k_ref, vt_ref, lq1_ref, lk1_ref, lq2_ref, lk2_ref, sg_ref, o_ctx_ref, o_lat_ref = refs
    j = pl.program_id(2)
    lam = (jnp.exp(jnp.sum(lq1_ref[...] * lk1_ref[...], axis=-1, keepdims=True))
           - jnp.exp(jnp.sum(lq2_ref[...] * lk2_ref[...], axis=-1, keepdims=True)) + lam_init)
    tq = q_refs[0].shape[0]
    n_heads = q_refs[0].shape[1] // DIFF_PAIR
    lane = lax.broadcasted_iota(jnp.int32, (tq, DIFF_PAIR), 1)

    def attend(tiles, n_keys, o_ref):
        problems = []
        for q_ref in tiles:
            for hd in range(n_heads):
                sl = slice(hd * DIFF_PAIR, (hd + 1) * DIFF_PAIR)
                q = q_ref[:, sl].astype(F32)
                kk, vt = k_ref[0, :n_keys, sl], vt_ref[0, sl, :n_keys]
                problems.append((kk, jnp.where(lane < DIFF_HEAD_DIM, q, 0.0).astype(BF16), vt))
                problems.append((kk, jnp.where(lane >= DIFF_HEAD_DIM, q, 0.0).astype(BF16), vt))
        outs = _attend_t(problems, DIFF_LOOKAHEAD)
        for t in range(len(tiles)):
            for hd in range(n_heads):
                (o1, l1), (o2, l2) = outs[2 * (t * n_heads + hd)], outs[2 * (t * n_heads + hd) + 1]
                o = (o1 * (1.0 / l1) - o2 * (lam / l2)).T
                o_ref[t * tq:(t + 1) * tq, hd * DIFF_PAIR:(hd + 1) * DIFF_PAIR] = (
                    _rms(o, sg_ref[...]) * (1.0 - lam_init)).astype(BF16)

    @pl.when(j == 0)
    def _():
        attend(q_refs[:1], n_ctx, o_ctx_ref)

    @pl.when(j > 0)
    def _():
        attend(q_refs, k_ref.shape[1], o_lat_ref)


def _ffn(hs, ms, g_ref, wg_ref, wu_ref, wd_ref):
    tm = hs[0].shape[0]
    a = jnp.concatenate([_rms_mod(h, g_ref[...], m[3:4], m[4:5]).astype(BF16) for h, m in zip(hs, ms)], axis=0)
    project = lambda c: (_dot(a, wg_ref[:, c * COL_BLOCK:(c + 1) * COL_BLOCK]),
                         _dot(a, wu_ref[:, c * COL_BLOCK:(c + 1) * COL_BLOCK]))
    n_chunks = wg_ref.shape[1] // COL_BLOCK
    hid, nxt = [], project(0)
    for c in range(n_chunks):
        (gate, up), nxt = nxt, (project(c + 1) if c + 1 < n_chunks else None)
        hid.append((gate / (1.0 + jnp.exp(-gate)) * up).astype(BF16))
    down = _dot(jnp.concatenate(hid, axis=-1), wd_ref[...])
    return [h + m[5:6] * down[s * tm:(s + 1) * tm] for s, (h, m) in enumerate(zip(hs, ms))]


def _l0_mix_ffn_kernel(*refs, n_ctx, n_tok):
    x_refs, o_refs, refs = refs[:PROJ_SUBTILES], refs[PROJ_SUBTILES:2 * PROJ_SUBTILES], refs[2 * PROJ_SUBTILES:]
    ctx_ref, o_ctx_ref, u_ref, pw_ref, ps_ref, wo_ref, mod_ref, cmod_ref, g_ref, wg_ref, wu_ref, wd_ref, out_ref = refs
    tm = ROW_TILE
    win = 2 * tm
    ps = ps_ref[...]
    groups = [slice(g * POOL_GROUP, (g + 1) * POOL_GROUP) for g in range(len(POOL_WINDOWS))]
    first_is_ctx = pl.program_id(1) == 0
    o_all = jnp.concatenate([jnp.where(first_is_ctx, o_ctx_ref[...], o_refs[0][...])]
                            + [o_ref[...] for o_ref in o_refs[1:]], axis=0)
    attn_part = _dot(o_all, wo_ref[POOL_WIDTH:, :])
    hs, ms, uts, means = [], [], [], []
    for s, x_ref in enumerate(x_refs):
        r = PROJ_SUBTILES * pl.program_id(1) + s
        h, m, seg_lo, seg_hi = x_ref[0], mod_ref[0, 0], n_ctx, n_tok
        ws = jnp.clip(r * tm - tm // 2, n_ctx, n_tok - win)
        if s == 0:
            is_ctx = r == 0
            h, m = jnp.where(is_ctx, ctx_ref[0], h), jnp.where(is_ctx, cmod_ref[0, 0], m)
            seg_lo, seg_hi, ws = jnp.where(is_ctx, 0, n_ctx), jnp.where(is_ctx, n_ctx, n_tok), jnp.where(is_ctx, 0, ws)
        hs.append(h)
        ms.append(m)
        uw = u_ref[0, pl.ds(pl.multiple_of(ws, LANES), win), :]
        uts.append(u_ref[0, pl.ds(pl.multiple_of(r * tm, tm), tm), :].astype(F32))
        row = r * tm + lax.broadcasted_iota(jnp.int32, (tm, 1), 0)
        col = ws + lax.broadcasted_iota(jnp.int32, (tm, win), 1)
        for sl, w in zip(groups, POOL_WINDOWS):
            lo = jnp.maximum(row - w // 2, seg_lo)
            hi = jnp.minimum(row - w // 2 + w, seg_hi)
            band = jnp.where(col >= lo, jnp.where(col < hi, 1.0, 0.0), 0.0).astype(BF16)
            means.append(_dot(band, uw[:, sl]) / (hi - lo).astype(F32))
    n_g = len(groups)
    ys = []
    for g, sl in enumerate(groups):
        dev = jnp.concatenate([means[s * n_g + g] - uts[s][:, sl] for s in range(len(hs))], axis=0)
        ys.append((_dot(dev.astype(BF16), pw_ref[g]) * ps[:, sl]).astype(BF16))
    mixed = _dot(jnp.concatenate(ys, axis=-1), wo_ref[:POOL_WIDTH, :]) + attn_part
    h1 = [h + m[2:3] * mixed[s * tm:(s + 1) * tm] for s, (h, m) in enumerate(zip(hs, ms))]
    for s, out in enumerate(_ffn(h1, ms, g_ref, wg_ref, wu_ref, wd_ref)):
        out_ref[s * tm:(s + 1) * tm, :] = out


def _l1_proj_kernel(h_ref, mod_ref, cmod_ref, g_ref, wdq_ref, qlg_ref, wuq_ref, wdkv_ref, kvg_ref, wukv_ref,
                    qg_ref, kg_ref, cos_ref, sa_ref, sb_ref, ones_ref, q_ref, k_ref, vt_ref):
    subs = []
    for s in range(h_ref.shape[0] // ROW_TILE):
        is_ctx = (pl.program_id(1) == 0) if s == 0 else False
        m = jnp.where(is_ctx, cmod_ref[0, 0], mod_ref[0, 0]) if s == 0 else mod_ref[0, 0]
        subs.append(_rms_mod(h_ref[s * ROW_TILE:(s + 1) * ROW_TILE, :], g_ref[...], m[0:1], m[1:2]).astype(BF16))
    a = jnp.concatenate(subs, axis=0)
    tables, rope = _rope_fns(cos_ref[...], sa_ref[...], sb_ref[...], MLA_ROPE // 2)

    def head_sumsq(z):
        return _dot(jnp.square(z).astype(BF16), ones_ref[...])

    cq = _rms(_dot(a, wdq_ref[...]), qlg_ref[...]).astype(BF16)
    ckv = _dot(a, wdkv_ref[...])
    kr = ckv[:, MLA_KV_RANK:]
    ckvn = _rms(ckv[:, :MLA_KV_RANK], kvg_ref[...]).astype(BF16)
    root_n, n_eps = MLA_QK ** 0.5, MLA_QK * EPS
    tq = tables(qg_ref[...] * (root_n * MLA_QK ** -0.5 * LOG2E))
    kg = kg_ref[...] * root_n
    krr = rope(kr, tables(kg))
    kr_ss = _dot(jnp.square(kr).astype(BF16), ones_ref[:LANES, :LANES]) + n_eps
    heads = [(i, slice(i * LANES, (i + 1) * LANES)) for i in range(COL_BLOCK // LANES)]

    def q_pair(p, z):
        r = lax.rsqrt(head_sumsq(z) + n_eps)
        for i, sl in heads:
            q_ref[:, p * COL_BLOCK + i * LANES:p * COL_BLOCK + (i + 1) * LANES] = (rope(z[:, sl], tq) * r[:, sl]).astype(BF16)

    def k_pair(p, z):
        ss = head_sumsq(z)
        for i, sl in heads:
            rk = lax.rsqrt(ss[:, sl] + kr_ss)
            k_ref[:, p * COL_BLOCK + i * LANES:p * COL_BLOCK + (i + 1) * LANES] = (rk * (z[:, sl] * kg + krr)).astype(BF16)

    def v_pair(p, z):
        vt_ref[0, p * COL_BLOCK:(p + 1) * COL_BLOCK, :] = z.T.astype(BF16)

    n_kp = MLA_HEADS * LANES // COL_BLOCK
    n_vp = MLA_HEADS * MLA_V // COL_BLOCK
    items = []
    for p in range(n_kp):
        items += [(q_pair, p, cq, wuq_ref, p), (k_pair, p, ckvn, wukv_ref, p)]
        if p % (n_kp // n_vp) == 0:
            items.append((v_pair, p // (n_kp // n_vp), ckvn, wukv_ref, n_kp + p // (n_kp // n_vp)))
    _project_columns(items)


def _mla_attn_kernel(*refs):
    q_refs, (k_ref, vt_ref, o_ref) = refs[:MLA_QTILES_PER_STEP], refs[MLA_QTILES_PER_STEP:]
    n_heads, tq = k_ref.shape[2] // LANES, q_refs[0].shape[0]
    problems = []
    for q_ref in q_refs:
        for i in range(n_heads):
            sl = slice(i * LANES, (i + 1) * LANES)
            problems.append((k_ref[0, :, sl], q_ref[:, sl], vt_ref[0, i * MLA_V:(i + 1) * MLA_V, :]))
    outs = [ot * (1.0 / l) for ot, l in _attend_t(problems, MLA_LOOKAHEAD)]
    for t in range(len(q_refs)):
        o_ref[t * tq:(t + 1) * tq, :] = jnp.concatenate(outs[t * n_heads:(t + 1) * n_heads], axis=0).T.astype(BF16)


def _l1_mix_ffn_kernel(*refs):
    h_refs, (o_ref, wo_ref, mod_ref, g_ref, wg_ref, wu_ref, wd_ref, out_ref) = refs[:FFN_SUBTILES], refs[FFN_SUBTILES:]
    m = mod_ref[0, 0]
    tm = h_refs[0].shape[0]
    mixed = _dot(o_ref[...], wo_ref[...])
    h1 = [h_ref[...] + m[2:3] * mixed[s * tm:(s + 1) * tm] for s, h_ref in enumerate(h_refs)]
    for s, out in enumerate(_ffn(h1, [m] * len(h1), g_ref, wg_ref, wu_ref, wd_ref)):
        out_ref[s * tm:(s + 1) * tm, :] = out


def _rope_tables(n_ctx, n_lat, rot_dim, first_lane):
    n_freq = rot_dim // 4
    half = rot_dim // 2
    freqs = ROPE_THETA ** (-jnp.arange(n_freq, dtype=F32) / n_freq)
    rows = n_lat // GRID_W
    row = jnp.repeat(jnp.arange(rows, dtype=F32), GRID_W)
    col = jnp.tile(jnp.arange(GRID_W, dtype=F32), rows)
    ang = jnp.concatenate([row[:, None] * freqs, col[:, None] * freqs], axis=-1)
    ang = jnp.concatenate([jnp.zeros((n_ctx, half), F32), ang], axis=0)
    cos_h, sin_h = jnp.cos(ang), jnp.sin(ang)
    zero = jnp.zeros_like(sin_h)
    n_rep = (LANES - first_lane) // rot_dim if first_lane == 0 else 1
    cos = jnp.concatenate([cos_h, cos_h] * n_rep, axis=-1)
    sa = jnp.concatenate([-sin_h, zero] * n_rep, axis=-1)
    sb = jnp.concatenate([zero, sin_h] * n_rep, axis=-1)
    n_rows = n_ctx + n_lat
    pad_lo = first_lane
    pad_hi = LANES - first_lane - cos.shape[1]
    cos = jnp.concatenate([jnp.ones((n_rows, pad_lo), F32), cos, jnp.ones((n_rows, pad_hi), F32)], axis=-1)
    sa = jnp.pad(sa, ((0, 0), (pad_lo, pad_hi)))
    sb = jnp.pad(sb, ((0, 0), (pad_lo, pad_hi)))
    return cos, sa, sb


def _pad_heads(w, width):
    k = w.shape[0]
    w = w.reshape(k, MLA_HEADS, width)
    return jnp.pad(w, ((0, 0), (0, 0), (0, LANES - width))).reshape(k, MLA_HEADS * LANES)


def kernel(x, c, ctx, c_ctx, mod_w, mod_b, norm_mix_g, norm_ffn_g, ffn_w_gate, ffn_w_up, ffn_w_down, ab_w_in, ab_w_out, pool_w, pool_scale, diff_q_norm_g, diff_k_norm_g, diff_lam_q1, diff_lam_k1, diff_lam_q2, diff_lam_k2, diff_subln_g, mla_w_dq, mla_q_lat_g, mla_w_uq, mla_w_dkv, mla_kv_lat_g, mla_w_ukv, mla_q_norm_g, mla_k_norm_g, mla_w_out):
    n_batch, n_lat, d = x.shape
    n_ctx = ctx.shape[1]
    n_tok = n_ctx + n_lat
    tm = ROW_TILE
    assert n_ctx == tm and n_lat % tm == 0 and n_lat >= 2 * tm and n_batch < MOD_ROWS
    assert mod_w.shape[0] == 2 and ab_w_in.shape[0] == 1 and mla_w_dq.shape[0] == 1
    tpb = n_tok // tm
    lpb = n_lat // tm
    n_rows = n_batch * n_tok
    ffn_hidden = ffn_w_gate.shape[-1]
    ctx_mod = n_batch

    cond = jnp.concatenate([c, c_ctx[None, :], jnp.zeros((MOD_ROWS - n_batch - 1, d), F32)], axis=0)
    mods = _adaln(cond, mod_w, mod_b)

    def lat_row(t):
        return (t // lpb) * tpb + 1 + t % lpb

    vec = lambda v: v.reshape(1, -1)
    tp = PROJ_SUBTILES * tm
    assert n_tok % tp == 0
    ppb = n_tok // tp
    pb_row = lambda w: pl.BlockSpec((tp, w), lambda b, j: (b * ppb + j, 0))
    pb_tab = pl.BlockSpec((tp, LANES), lambda b, j: (j, 0))
    pb_mod = lambda l: pl.BlockSpec((1, 1, 6, d), lambda b, j: (l, b, 0, 0))
    pb_cmod = lambda l: pl.BlockSpec((1, 1, 6, d), lambda b, j: (l, ctx_mod, 0, 0))
    tf = FFN_SUBTILES * tm
    assert n_lat % tf == 0
    x_specs = [pl.BlockSpec((1, tm, d), lambda b, j, s=s: (b, jnp.maximum(PROJ_SUBTILES * j + s - 1, 0), 0))
               for s in range(PROJ_SUBTILES)]
    ctx_spec = pl.BlockSpec((1, tm, d), lambda b, j: (b, 0, 0))
    o_specs = [pl.BlockSpec((tm, DIFF_WIDTH), lambda b, j, s=s: (b * lpb + jnp.maximum(PROJ_SUBTILES * j + s - 1, 0), 0))
               for s in range(PROJ_SUBTILES)]
    ffn_w_specs = [_const_spec((1, d)), _const_spec((d, ffn_hidden)), _const_spec((d, ffn_hidden)),
                   _const_spec((ffn_hidden, d))]

    def ffn_weights(layer):
        return (vec(norm_ffn_g[layer]), ffn_w_gate[layer].astype(BF16), ffn_w_up[layer].astype(BF16),
                ffn_w_down[layer].astype(BF16))

    lam_init = 0.8 - 0.6 * math.exp(-0.3 * 0)
    cos, sa, sb = _rope_tables(n_ctx, n_lat, DIFF_HEAD_DIM, 0)
    grp = jnp.arange(COL_BLOCK) // DIFF_HEAD_DIM
    block_diag = (grp[:, None] == grp[None, :]).astype(BF16)
    qkv_shape = jax.ShapeDtypeStruct((n_rows, DIFF_WIDTH), BF16)
    u, q, k, vt = pl.pallas_call(
        _l0_proj_kernel,
        grid=(n_batch, ppb),
        in_specs=x_specs + [ctx_spec, pb_mod(0), pb_cmod(0), _const_spec((1, d)), _const_spec(ab_w_in.shape[1:]),
                    _const_spec((1, LANES)), _const_spec((1, LANES)),
                    _const_spec((COL_BLOCK, COL_BLOCK)), pb_tab, pb_tab, pb_tab],
        out_specs=[pb_row(POOL_WIDTH), pb_row(DIFF_WIDTH), pb_row(DIFF_WIDTH),
                   pl.BlockSpec((1, DIFF_WIDTH, tp), lambda b, j: (b, 0, j))],
        out_shape=[jax.ShapeDtypeStruct((n_rows, POOL_WIDTH), BF16), qkv_shape, qkv_shape,
                   jax.ShapeDtypeStruct((n_batch, DIFF_WIDTH, n_tok), BF16)],
        compiler_params=_params("arbitrary", "arbitrary"),
        name="l0_proj",
    )(*([x] * PROJ_SUBTILES), ctx, mods, mods, vec(norm_mix_g[0]), ab_w_in[0].astype(BF16),
      vec(jnp.tile(diff_q_norm_g[0], LANES // DIFF_HEAD_DIM)), vec(jnp.tile(diff_k_norm_g[0], LANES // DIFF_HEAD_DIM)),
      block_diag, cos, sa, sb)

    dw = DIFF_HEADS_PER_STEP * DIFF_PAIR
    k_spec = pl.BlockSpec((1, n_tok, dw), lambda b, hd, j: (b, 0, hd))
    v_spec = pl.BlockSpec((1, dw, n_tok), lambda b, hd, j: (b, hd, 0))
    lam_spec = pl.BlockSpec((1, DIFF_HEAD_DIM), lambda b, hd, j: (0, 0))
    dq = DIFF_QTILES_PER_STEP
    assert lpb % dq == 0
    q_specs = [pl.BlockSpec((tm, dw), lambda b, hd, j, t=t: (b * tpb + jnp.maximum(dq * (j - 1) + 1, 0) + t, hd))
               for t in range(dq)]
    o_ctx, o_lat = pl.pallas_call(
        functools.partial(_diff_attn_kernel, n_ctx=n_ctx, lam_init=lam_init),
        grid=(n_batch, DIFF_HEADS // DIFF_HEADS_PER_STEP, 1 + lpb // dq),
        in_specs=q_specs + [k_spec, v_spec, lam_spec, lam_spec, lam_spec, lam_spec,
                            pl.BlockSpec((1, DIFF_PAIR), lambda b, hd, j: (0, 0))],
        out_specs=[pl.BlockSpec((tm, dw), lambda b, hd, j: (b, hd)),
                   pl.BlockSpec((dq * tm, dw), lambda b, hd, j: (b * (lpb // dq) + jnp.maximum(j - 1, 0), hd))],
        out_shape=[jax.ShapeDtypeStruct((n_batch * n_ctx, DIFF_WIDTH), BF16),
                   jax.ShapeDtypeStruct((n_batch * n_lat, DIFF_WIDTH), BF16)],
        compiler_params=_params("arbitrary", "arbitrary", "arbitrary"),
        name="diff_attn",
    )(*([q] * dq), k.reshape(n_batch, n_tok, DIFF_WIDTH), vt,
      vec(diff_lam_q1[0]), vec(diff_lam_k1[0]), vec(diff_lam_q2[0]), vec(diff_lam_k2[0]), vec(diff_subln_g[0]))

    h = pl.pallas_call(
        functools.partial(_l0_mix_ffn_kernel, n_ctx=n_ctx, n_tok=n_tok),
        grid=(n_batch, ppb),
        in_specs=x_specs + o_specs + [ctx_spec, pl.BlockSpec((tm, DIFF_WIDTH), lambda b, j: (b, 0)),
                                      pl.BlockSpec((1, n_tok, POOL_WIDTH), lambda b, j: (b, 0, 0)),
                            _const_spec(pool_w.shape[1:]), _const_spec((1, POOL_WIDTH)),
                            _const_spec(ab_w_out.shape[1:]), pb_mod(0), pb_cmod(0)] + ffn_w_specs,
        out_specs=pb_row(d),
        out_shape=jax.ShapeDtypeStruct((n_rows, d), F32),
        compiler_params=_params("arbitrary", "arbitrary"),
        name="l0_mix_ffn",
    )(*([x] * PROJ_SUBTILES), *([o_lat] * PROJ_SUBTILES), ctx, o_ctx, u.reshape(n_batch, n_tok, POOL_WIDTH),
      pool_w[0].astype(BF16),
      vec(pool_scale[0]), ab_w_out[0].astype(BF16), mods, mods, *ffn_weights(0))

    cos, sa, sb = _rope_tables(n_ctx, n_lat, MLA_ROPE, MLA_NOPE)
    w_uq = _pad_heads(mla_w_uq[0], MLA_QK).astype(BF16)
    w_dkv = mla_w_dkv[0]
    w_dkv = jnp.concatenate([
        w_dkv[:, :MLA_KV_RANK], jnp.zeros((d, MLA_NOPE), F32), w_dkv[:, MLA_KV_RANK:],
        jnp.zeros((d, LANES - MLA_QK), F32)], axis=-1).astype(BF16)
    w_ukv = mla_w_ukv[0].reshape(MLA_KV_RANK, MLA_HEADS, MLA_NOPE + MLA_V)
    w_ukv = jnp.concatenate([
        _pad_heads(w_ukv[:, :, :MLA_NOPE].reshape(MLA_KV_RANK, -1), MLA_NOPE),
        w_ukv[:, :, MLA_NOPE:].reshape(MLA_KV_RANK, -1)], axis=-1).astype(BF16)
    pad_gain = lambda g: vec(jnp.pad(g, (0, LANES - MLA_QK)))
    head_of = jnp.arange(2 * LANES) // LANES
    head_ones = (head_of[:, None] == head_of[None, :]).astype(BF16)
    hq = MLA_HEADS * LANES
    q, k, vt = pl.pallas_call(
        _l1_proj_kernel,
        grid=(n_batch, ppb),
        in_specs=[pb_row(d), pb_mod(1), pb_cmod(1), _const_spec((1, d)), _const_spec(mla_w_dq.shape[1:]),
                  _const_spec((1, mla_w_dq.shape[2])), _const_spec(w_uq.shape), _const_spec(w_dkv.shape),
                  _const_spec((1, MLA_KV_RANK)), _const_spec(w_ukv.shape), _const_spec((1, LANES)),
                  _const_spec((1, LANES)), pb_tab, pb_tab, pb_tab, _const_spec((2 * LANES, 2 * LANES))],
        out_specs=[pb_row(hq), pb_row(hq), pl.BlockSpec((1, MLA_HEADS * MLA_V, tp), lambda b, j: (b, 0, j))],
        out_shape=[jax.ShapeDtypeStruct((n_rows, hq), BF16), jax.ShapeDtypeStruct((n_rows, hq), BF16),
                   jax.ShapeDtypeStruct((n_batch, MLA_HEADS * MLA_V, n_tok), BF16)],
        compiler_params=_params("arbitrary", "arbitrary"),
        name="l1_proj",
    )(h, mods, mods, vec(norm_mix_g[1]), mla_w_dq[0].astype(BF16), vec(mla_q_lat_g[0]), w_uq, w_dkv,
      vec(mla_kv_lat_g[0]), w_ukv, pad_gain(mla_q_norm_g[0]), pad_gain(mla_k_norm_g[0]), cos, sa, sb,
      head_ones)

    n_lat_rows = n_batch * n_lat
    qpt = MLA_QTILES_PER_STEP
    assert lpb % qpt == 0
    o = pl.pallas_call(
        _mla_attn_kernel,
        grid=(n_batch, MLA_HEADS // MLA_HEADS_PER_STEP, lpb // qpt),
        in_specs=[pl.BlockSpec((tm, MLA_HEADS_PER_STEP * LANES), lambda b, hp, j, t=t: (b * tpb + 1 + qpt * j + t, hp))
                  for t in range(qpt)]
                 + [pl.BlockSpec((1, n_tok, MLA_HEADS_PER_STEP * LANES), lambda b, hp, j: (b, 0, hp)),
                    pl.BlockSpec((1, MLA_HEADS_PER_STEP * MLA_V, n_tok), lambda b, hp, j: (b, hp, 0))],
        out_specs=pl.BlockSpec((qpt * tm, MLA_HEADS_PER_STEP * MLA_V), lambda b, hp, j: (b * (lpb // qpt) + j, hp)),
        out_shape=jax.ShapeDtypeStruct((n_lat_rows, MLA_HEADS * MLA_V), BF16),
        compiler_params=_params("arbitrary", "arbitrary", "arbitrary"),
        name="mla_attn",
    )(*([q] * qpt), k.reshape(n_batch, n_tok, hq), vt)

    h = pl.pallas_call(
        _l1_mix_ffn_kernel,
        grid=(n_lat_rows // tf,),
        in_specs=[pl.BlockSpec((tm, d), lambda t, s=s: (lat_row(FFN_SUBTILES * t + s), 0)) for s in range(FFN_SUBTILES)]
                 + [pl.BlockSpec((tf, MLA_HEADS * MLA_V), lambda t: (t, 0)), _const_spec(mla_w_out.shape[1:]),
                    pl.BlockSpec((1, 1, 6, d), lambda t: (1, t * tf // n_lat, 0, 0))] + ffn_w_specs,
        out_specs=pl.BlockSpec((tf, d), lambda t: (t, 0)),
        out_shape=jax.ShapeDtypeStruct((n_lat_rows, d), F32),
        compiler_params=_params("arbitrary"),
        name="l1_mix_ffn",
    )(*([h] * FFN_SUBTILES), o, mla_w_out[0].astype(BF16), mods, *ffn_weights(1))
    return h.reshape(n_batch, n_lat, d)
```

```python
import functools
import math

import jax
import jax.numpy as jnp
from jax import lax
from jax.experimental import pallas as pl
from jax.experimental.pallas import tpu as pltpu

F32 = jnp.float32
BF16 = jnp.bfloat16

EPS = 1e-6
GRID_W = 64
ROPE_THETA = 10000.0
POOL_WINDOWS = (2, 4, 8, 16)
POOL_GROUP = 128
POOL_WIDTH = POOL_GROUP * len(POOL_WINDOWS)
DIFF_HEADS = 4
DIFF_HEAD_DIM = 64
DIFF_PAIR = 2 * DIFF_HEAD_DIM
DIFF_WIDTH = DIFF_HEADS * DIFF_PAIR
MLA_HEADS = 16
MLA_NOPE = 64
MLA_ROPE = 32
MLA_QK = MLA_NOPE + MLA_ROPE
MLA_V = 64
MLA_KV_RANK = 256
LANES = 128
ROW_TILE = 256
COL_BLOCK = 256
PROJ_SUBTILES = 3
FFN_SUBTILES = 2
MOD_ROWS = 16
ONES_ROWS = 16
LOG2E = math.log2(math.e)
DIFF_HEADS_PER_STEP = 4
MLA_HEADS_PER_STEP = 16
MLA_QTILES_PER_STEP = 2
DIFF_QTILES_PER_STEP = 2
KEY_CHUNK = 256
PROJECT_LOOKAHEAD = 3
DIFF_LOOKAHEAD = 3
MLA_LOOKAHEAD = 1
VMEM_LIMIT = 56 * 1024 * 1024

_NT = (((1,), (1,)), ((), ()))


def _dot(a, b):
    return jnp.dot(a, b, preferred_element_type=F32)


def _rms(x, g):
    return x * lax.rsqrt(jnp.mean(x * x, axis=-1, keepdims=True) + EPS) * g


def _rms_mod(x, g, shift, scale):
    return x * lax.rsqrt(jnp.mean(x * x, axis=-1, keepdims=True) + EPS) * (g * (1.0 + scale)) + shift


def _params(*sem, fuse_inputs=None):
    return pltpu.CompilerParams(dimension_semantics=sem, vmem_limit_bytes=VMEM_LIMIT, allow_input_fusion=fuse_inputs)


def _const_spec(shape):
    zeros = (0,) * len(shape)
    return pl.BlockSpec(shape, lambda *_: zeros, pipeline_mode=pl.Buffered(1))


def _adaln_kernel(cond_ref, w_ref, b_ref, o_ref):
    c = cond_ref[...]
    a = (c / (1.0 + jnp.exp(-c))).astype(BF16)
    o_ref[0, 0] = _dot(a, w_ref[0].astype(BF16)) + b_ref[0, 0]


def _adaln(cond, mod_w, mod_b):
    depth, d, d6 = mod_w.shape
    n = d6 // d
    out = pl.pallas_call(
        _adaln_kernel,
        grid=(depth, n),
        in_specs=[
            pl.BlockSpec((MOD_ROWS, d), lambda l, j: (0, 0)),
            pl.BlockSpec((1, d, d), lambda l, j: (l, 0, j)),
            pl.BlockSpec((1, 1, 1, d), lambda l, j: (l, j, 0, 0)),
        ],
        out_specs=pl.BlockSpec((1, 1, MOD_ROWS, d), lambda l, j: (l, j, 0, 0)),
        out_shape=jax.ShapeDtypeStruct((depth, n, MOD_ROWS, d), F32),
        compiler_params=_params("arbitrary", "arbitrary"),
        name="adaln",
    )(cond, mod_w, mod_b.reshape(depth, n, 1, d))
    return out.transpose(0, 2, 1, 3)


def _project_columns(items):
    project = lambda it: _dot(it[2], it[3][:, it[4] * COL_BLOCK:(it[4] + 1) * COL_BLOCK])
    zs = []
    for i, it in enumerate(items):
        while len(zs) < min(i + 1 + PROJECT_LOOKAHEAD, len(items)):
            zs.append(project(items[len(zs)]))
        it[0](it[1], zs[i])
        zs[i] = None


def _rope_fns(cos, sa, sb, half):
    def tables(g):
        gb = jnp.broadcast_to(g, cos.shape)
        return gb * cos, pltpu.roll(gb, LANES - half, 1) * sa, pltpu.roll(gb, half, 1) * sb

    def rope(z, t):
        return z * t[0] + pltpu.roll(z, LANES - half, 1) * t[1] + pltpu.roll(z, half, 1) * t[2]

    return tables, rope


def _l0_proj_kernel(*refs):
    x_refs, refs = refs[:PROJ_SUBTILES], refs[PROJ_SUBTILES:]
    (ctx_ref, mod_ref, cmod_ref, g_ref, w_ref, qg_ref, kg_ref, bd_ref, cos_ref, sa_ref, sb_ref,
     u_ref, q_ref, k_ref, vt_ref) = refs
    subs = []
    for s, x_ref in enumerate(x_refs):
        h, m = x_ref[0], mod_ref[0, 0]
        if s == 0:
            is_ctx = pl.program_id(1) == 0
            h, m = jnp.where(is_ctx, ctx_ref[0], h), jnp.where(is_ctx, cmod_ref[0, 0], m)
        subs.append(_rms_mod(h, g_ref[...], m[0:1], m[1:2]).astype(BF16))
    a = jnp.concatenate(subs, axis=0)
    tables, rope = _rope_fns(cos_ref[...], sa_ref[...], sb_ref[...], DIFF_HEAD_DIM // 2)
    root_n, n_eps = DIFF_HEAD_DIM ** 0.5, DIFF_HEAD_DIM * EPS
    tq = tables(qg_ref[...] * (root_n * DIFF_HEAD_DIM ** -0.5 * LOG2E))
    tk = tables(kg_ref[...] * root_n)

    def plain(ref):
        def finish(p, z):
            ref[:, p * COL_BLOCK:(p + 1) * COL_BLOCK] = z.astype(BF16)
        return finish

    def norm_rope(ref, t):
        def finish(p, z):
            r = lax.rsqrt(_dot(jnp.square(z).astype(BF16), bd_ref[...]) + n_eps)
            for i in range(COL_BLOCK // LANES):
                sl = slice(i * LANES, (i + 1) * LANES)
                ref[:, p * COL_BLOCK + i * LANES:p * COL_BLOCK + (i + 1) * LANES] = (rope(z[:, sl], t) * r[:, sl]).astype(BF16)
        return finish

    def transposed(p, z):
        vt_ref[0, p * COL_BLOCK:(p + 1) * COL_BLOCK, :] = z.T.astype(BF16)

    items, col = [], 0
    for finish, width in ((plain(u_ref), POOL_WIDTH), (norm_rope(q_ref, tq), DIFF_WIDTH),
                          (norm_rope(k_ref, tk), DIFF_WIDTH), (transposed, DIFF_WIDTH)):
        items += [(finish, p, a, w_ref, col + p) for p in range(width // COL_BLOCK)]
        col += width // COL_BLOCK
    _project_columns(items)


def _attend_t(problems, lookahead):
    def scores(p):
        return lax.dot_general(p[0], p[1], _NT, preferred_element_type=F32)

    def finish(st, vt):
        n, dv = st.shape[0], vt.shape[0]
        rs, ms = [], []
        for lo in range(0, n, KEY_CHUNK):
            hi = min(lo + KEY_CHUNK, n)
            sc = st[lo:hi]
            ms.append(jnp.max(sc, axis=0, keepdims=True))
            e = jnp.exp2(sc - ms[-1]).astype(BF16)
            lhs = jnp.concatenate([vt[:, lo:hi], jnp.ones((ONES_ROWS, hi - lo), BF16)], axis=0)
            rs.append(_dot(lhs, e))
        m_all = functools.reduce(jnp.maximum, ms)
        r = functools.reduce(jnp.add, [rc * jnp.exp2(mc - m_all) for rc, mc in zip(rs, ms)])
        return r[:dv], r[dv:dv + 1]

    outs, sts = [], []
    for i, p in enumerate(problems):
        while len(sts) < min(i + 1 + lookahead, len(problems)):
            sts.append(scores(problems[len(sts)]))
        outs.append(finish(sts[i], p[2]))
    return outs


def _diff_attn_kernel(*refs, n_ctx, lam_init):
    q_refs, refs = refs[:DIFF_QTILES_PER_STEP], refs[DIFF_QTILES_PER_STEP:]
    k_ref, vt_ref, lq1_ref, lk1_ref, lq2_ref, lk2_ref, sg_ref, o_ctx_ref, o_lat_ref = refs
    j = pl.program_id(2)
    lam = (jnp.exp(jnp.sum(lq1_ref[...] * lk1_ref[...], axis=-1, keepdims=True))
           - jnp.exp(jnp.sum(lq2_ref[...] * lk2_ref[...], axis=-1, keepdims=True)) + lam_init)
    tq = q_refs[0].shape[0]
    n_heads = q_refs[0].shape[1] // DIFF_PAIR
    lane = lax.broadcasted_iota(jnp.int32, (tq, DIFF_PAIR), 1)

    def attend(tiles, n_keys, o_ref):
        problems = []
        for q_ref in tiles:
            for hd in range(n_heads):
                sl = slice(hd * DIFF_PAIR, (hd + 1) * DIFF_PAIR)
                q = q_ref[:, sl].astype(F32)
                kk, vt = k_ref[0, :n_keys, sl], vt_ref[0, sl, :n_keys]
                problems.append((kk, jnp.where(lane < DIFF_HEAD_DIM, q, 0.0).astype(BF16), vt))
                problems.append((kk, jnp.where(lane >= DIFF_HEAD_DIM, q, 0.0).astype(BF16), vt))
        outs = _attend_t(problems, DIFF_LOOKAHEAD)
        for t in range(len(tiles)):
            for hd in range(n_heads):
                (o1, l1), (o2, l2) = outs[2 * (t * n_heads + hd)], outs[2 * (t * n_heads + hd) + 1]
                o = (o1 * (1.0 / l1) - o2 * (lam / l2)).T
                o_ref[t * tq:(t + 1) * tq, hd * DIFF_PAIR:(hd + 1) * DIFF_PAIR] = (
                    _rms(o, sg_ref[...]) * (1.0 - lam_init)).astype(BF16)

    @pl.when(j == 0)
    def _():
        attend(q_refs[:1], n_ctx, o_ctx_ref)

    @pl.when(j > 0)
    def _():
        attend(q_refs, k_ref.shape[1], o_lat_ref)


def _ffn(hs, ms, g_ref, wg_ref, wu_ref, wd_ref):
    tm = hs[0].shape[0]
    a = jnp.concatenate([_rms_mod(h, g_ref[...], m[3:4], m[4:5]).astype(BF16) for h, m in zip(hs, ms)], axis=0)
    project = lambda c: (_dot(a, wg_ref[:, c * COL_BLOCK:(c + 1) * COL_BLOCK]),
                         _dot(a, wu_ref[:, c * COL_BLOCK:(c + 1) * COL_BLOCK]))
    n_chunks = wg_ref.shape[1] // COL_BLOCK
    hid, nxt = [], project(0)
    for c in range(n_chunks):
        (gate, up), nxt = nxt, (project(c + 1) if c + 1 < n_chunks else None)
        hid.append((gate / (1.0 + jnp.exp(-gate)) * up).astype(BF16))
    down = _dot(jnp.concatenate(hid, axis=-1), wd_ref[...])
    return [h + m[5:6] * down[s * tm:(s + 1) * tm] for s, (h, m) in enumerate(zip(hs, ms))]


def _l0_mix_ffn_kernel(*refs, n_ctx, n_tok):
    x_refs, o_refs, refs = refs[:PROJ_SUBTILES], refs[PROJ_SUBTILES:2 * PROJ_SUBTILES], refs[2 * PROJ_SUBTILES:]
    ctx_ref, o_ctx_ref, u_ref, pw_ref, ps_ref, wo_ref, mod_ref, cmod_ref, g_ref, wg_ref, wu_ref, wd_ref, out_ref = refs
    tm = ROW_TILE
    win = 2 * tm
    ps = ps_ref[...]
    groups = [slice(g * POOL_GROUP, (g + 1) * POOL_GROUP) for g in range(len(POOL_WINDOWS))]
    first_is_ctx = pl.program_id(1) == 0
    o_all = jnp.concatenate([jnp.where(first_is_ctx, o_ctx_ref[...], o_refs[0][...])]
                            + [o_ref[...] for o_ref in o_refs[1:]], axis=0)
    attn_part = _dot(o_all, wo_ref[POOL_WIDTH:, :])
    hs, ms, uts, means = [], [], [], []
    for s, x_ref in enumerate(x_refs):
        r = PROJ_SUBTILES * pl.program_id(1) + s
        h, m, seg_lo, seg_hi = x_ref[0], mod_ref[0, 0], n_ctx, n_tok
        ws = jnp.clip(r * tm - tm // 2, n_ctx, n_tok - win)
        if s == 0:
            is_ctx = r == 0
            h, m = jnp.where(is_ctx, ctx_ref[0], h), jnp.where(is_ctx, cmod_ref[0, 0], m)
            seg_lo, seg_hi, ws = jnp.where(is_ctx, 0, n_ctx), jnp.where(is_ctx, n_ctx, n_tok), jnp.where(is_ctx, 0, ws)
        hs.append(h)
        ms.append(m)
        uw = u_ref[0, pl.ds(pl.multiple_of(ws, LANES), win), :]
        uts.append(u_ref[0, pl.ds(pl.multiple_of(r * tm, tm), tm), :].astype(F32))
        row = r * tm + lax.broadcasted_iota(jnp.int32, (tm, 1), 0)
        col = ws + lax.broadcasted_iota(jnp.int32, (tm, win), 1)
        for sl, w in zip(groups, POOL_WINDOWS):
            lo = jnp.maximum(row - w // 2, seg_lo)
            hi = jnp.minimum(row - w // 2 + w, seg_hi)
            band = jnp.where(col >= lo, jnp.where(col < hi, 1.0, 0.0), 0.0).astype(BF16)
            means.append(_dot(band, uw[:, sl]) / (hi - lo).astype(F32))
    n_g = len(groups)
    ys = []
    for g, sl in enumerate(groups):
        dev = jnp.concatenate([means[s * n_g + g] - uts[s][:, sl] for s in range(len(hs))], axis=0)
        ys.append((_dot(dev.astype(BF16), pw_ref[g]) * ps[:, sl]).astype(BF16))
    mixed = _dot(jnp.concatenate(ys, axis=-1), wo_ref[:POOL_WIDTH, :]) + attn_part
    h1 = [h + m[2:3] * mixed[s * tm:(s + 1) * tm] for s, (h, m) in enumerate(zip(hs, ms))]
    for s, out in enumerate(_ffn(h1, ms, g_ref, wg_ref, wu_ref, wd_ref)):
        out_ref[s * tm:(s + 1) * tm, :] = out


def _l1_proj_kernel(h_ref, mod_ref, cmod_ref, g_ref, wdq_ref, qlg_ref, wuq_ref, wdkv_ref, kvg_ref, wukv_ref,
                    qg_ref, kg_ref, cos_ref, sa_ref, sb_ref, ones_ref, q_ref, k_ref, vt_ref):
    subs = []
    for s in range(h_ref.shape[0] // ROW_TILE):
        is_ctx = (pl.program_id(1) == 0) if s == 0 else False
        m = jnp.where(is_ctx, cmod_ref[0, 0], mod_ref[0, 0]) if s == 0 else mod_ref[0, 0]
        subs.append(_rms_mod(h_ref[s * ROW_TILE:(s + 1) * ROW_TILE, :], g_ref[...], m[0:1], m[1:2]).astype(BF16))
    a = jnp.concatenate(subs, axis=0)
    tables, rope = _rope_fns(cos_ref[...], sa_ref[...], sb_ref[...], MLA_ROPE // 2)

    def head_sumsq(z):
        return _dot(jnp.square(z).astype(BF16), ones_ref[...])

    cq = _rms(_dot(a, wdq_ref[...]), qlg_ref[...]).astype(BF16)
    ckv = _dot(a, wdkv_ref[...])
    kr = ckv[:, MLA_KV_RANK:]
    ckvn = _rms(ckv[:, :MLA_KV_RANK], kvg_ref[...]).astype(BF16)
    root_n, n_eps = MLA_QK ** 0.5, MLA_QK * EPS
    tq = tables(qg_ref[...] * (root_n * MLA_QK ** -0.5 * LOG2E))
    kg = kg_ref[...] * root_n
    krr = rope(kr, tables(kg))
    kr_ss = _dot(jnp.square(kr).astype(BF16), ones_ref[:LANES, :LANES]) + n_eps
    heads = [(i, slice(i * LANES, (i + 1) * LANES)) for i in range(COL_BLOCK // LANES)]

    def q_pair(p, z):
        r = lax.rsqrt(head_sumsq(z) + n_eps)
        for i, sl in heads:
            q_ref[:, p * COL_BLOCK + i * LANES:p * COL_BLOCK + (i + 1) * LANES] = (rope(z[:, sl], tq) * r[:, sl]).astype(BF16)

    def k_pair(p, z):
        ss = head_sumsq(z)
        for i, sl in heads:
            rk = lax.rsqrt(ss[:, sl] + kr_ss)
            k_ref[:, p * COL_BLOCK + i * LANES:p * COL_BLOCK + (i + 1) * LANES] = (rk * (z[:, sl] * kg + krr)).astype(BF16)

    def v_pair(p, z):
        vt_ref[0, p * COL_BLOCK:(p + 1) * COL_BLOCK, :] = z.T.astype(BF16)

    n_kp = MLA_HEADS * LANES // COL_BLOCK
    n_vp = MLA_HEADS * MLA_V // COL_BLOCK
    items = []
    for p in range(n_kp):
        items += [(q_pair, p, cq, wuq_ref, p), (k_pair, p, ckvn, wukv_ref, p)]
        if p % (n_kp // n_vp) == 0:
            items.append((v_pair, p // (n_kp // n_vp), ckvn, wukv_ref, n_kp + p // (n_kp // n_vp)))
    _project_columns(items)


def _mla_attn_kernel(*refs):
    q_refs, (k_ref, vt_ref, o_ref) = refs[:MLA_QTILES_PER_STEP], refs[MLA_QTILES_PER_STEP:]
    n_heads, tq = k_ref.shape[2] // LANES, q_refs[0].shape[0]
    problems = []
    for q_ref in q_refs:
        for i in range(n_heads):
            sl = slice(i * LANES, (i + 1) * LANES)
            problems.append((k_ref[0, :, sl], q_ref[:, sl], vt_ref[0, i * MLA_V:(i + 1) * MLA_V, :]))
    outs = [ot * (1.0 / l) for ot, l in _attend_t(problems, MLA_LOOKAHEAD)]
    for t in range(len(q_refs)):
        o_ref[t * tq:(t + 1) * tq, :] = jnp.concatenate(outs[t * n_heads:(t + 1) * n_heads], axis=0).T.astype(BF16)


def _l1_mix_ffn_kernel(*refs):
    h_refs, (o_ref, wo_ref, mod_ref, g_ref, wg_ref, wu_ref, wd_ref, out_ref) = refs[:FFN_SUBTILES], refs[FFN_SUBTILES:]
    m = mod_ref[0, 0]
    tm = h_refs[0].shape[0]
    mixed = _dot(o_ref[...], wo_ref[...])
    h1 = [h_ref[...] + m[2:3] * mixed[s * tm:(s + 1) * tm] for s, h_ref in enumerate(h_refs)]
    for s, out in enumerate(_ffn(h1, [m] * len(h1), g_ref, wg_ref, wu_ref, wd_ref)):
        out_ref[s * tm:(s + 1) * tm, :] = out


def _rope_tables(n_ctx, n_lat, rot_dim, first_lane):
    n_freq = rot_dim // 4
    half = rot_dim // 2
    freqs = ROPE_THETA ** (-jnp.arange(n_freq, dtype=F32) / n_freq)
    rows = n_lat // GRID_W
    row = jnp.repeat(jnp.arange(rows, dtype=F32), GRID_W)
    col = jnp.tile(jnp.arange(GRID_W, dtype=F32), rows)
    ang = jnp.concatenate([row[:, None] * freqs, col[:, None] * freqs], axis=-1)
    ang = jnp.concatenate([jnp.zeros((n_ctx, half), F32), ang], axis=0)
    cos_h, sin_h = jnp.cos(ang), jnp.sin(ang)
    zero = jnp.zeros_like(sin_h)
    n_rep = (LANES - first_lane) // rot_dim if first_lane == 0 else 1
    cos = jnp.concatenate([cos_h, cos_h] * n_rep, axis=-1)
    sa = jnp.concatenate([-sin_h, zero] * n_rep, axis=-1)
    sb = jnp.concatenate([zero, sin_h] * n_rep, axis=-1)
    n_rows = n_ctx + n_lat
    pad_lo = first_lane
    pad_hi = LANES - first_lane - cos.shape[1]
    cos = jnp.concatenate([jnp.ones((n_rows, pad_lo), F32), cos, jnp.ones((n_rows, pad_hi), F32)], axis=-1)
    sa = jnp.pad(sa, ((0, 0), (pad_lo, pad_hi)))
    sb = jnp.pad(sb, ((0, 0), (pad_lo, pad_hi)))
    return cos, sa, sb


def _pad_heads(w, width):
    k = w.shape[0]
    w = w.reshape(k, MLA_HEADS, width)
    return jnp.pad(w, ((0, 0), (0, 0), (0, LANES - width))).reshape(k, MLA_HEADS * LANES)


def kernel(x, c, ctx, c_ctx, mod_w, mod_b, norm_mix_g, norm_ffn_g, ffn_w_gate, ffn_w_up, ffn_w_down, ab_w_in, ab_w_out, pool_w, pool_scale, diff_q_norm_g, diff_k_norm_g, diff_lam_q1, diff_lam_k1, diff_lam_q2, diff_lam_k2, diff_subln_g, mla_w_dq, mla_q_lat_g, mla_w_uq, mla_w_dkv, mla_kv_lat_g, mla_w_ukv, mla_q_norm_g, mla_k_norm_g, mla_w_out):
    n_batch, n_lat, d = x.shape
    n_ctx = ctx.shape[1]
    n_tok = n_ctx + n_lat
    tm = ROW_TILE
    assert n_ctx == tm and n_lat % tm == 0 and n_lat >= 2 * tm and n_batch < MOD_ROWS
    assert mod_w.shape[0] == 2 and ab_w_in.shape[0] == 1 and mla_w_dq.shape[0] == 1
    tpb = n_tok // tm
    lpb = n_lat // tm
    n_rows = n_batch * n_tok
    ffn_hidden = ffn_w_gate.shape[-1]
    ctx_mod = n_batch

    cond = jnp.concatenate([c, c_ctx[None, :], jnp.zeros((MOD_ROWS - n_batch - 1, d), F32)], axis=0)
    mods = _adaln(cond, mod_w, mod_b)

    def lat_row(t):
        return (t // lpb) * tpb + 1 + t % lpb

    vec = lambda v: v.reshape(1, -1)
    tp = PROJ_SUBTILES * tm
    assert n_tok % tp == 0
    ppb = n_tok // tp
    pb_row = lambda w: pl.BlockSpec((tp, w), lambda b, j: (b * ppb + j, 0))
    pb_tab = pl.BlockSpec((tp, LANES), lambda b, j: (j, 0))
    pb_mod = lambda l: pl.BlockSpec((1, 1, 6, d), lambda b, j: (l, b, 0, 0))
    pb_cmod = lambda l: pl.BlockSpec((1, 1, 6, d), lambda b, j: (l, ctx_mod, 0, 0))
    tf = FFN_SUBTILES * tm
    assert n_lat % tf == 0
    x_specs = [pl.BlockSpec((1, tm, d), lambda b, j, s=s: (b, jnp.maximum(PROJ_SUBTILES * j + s - 1, 0), 0))
               for s in range(PROJ_SUBTILES)]
    ctx_spec = pl.BlockSpec((1, tm, d), lambda b, j: (b, 0, 0))
    o_specs = [pl.BlockSpec((tm, DIFF_WIDTH), lambda b, j, s=s: (b * lpb + jnp.maximum(PROJ_SUBTILES * j + s - 1, 0), 0))
               for s in range(PROJ_SUBTILES)]
    ffn_w_specs = [_const_spec((1, d)), _const_spec((d, ffn_hidden)), _const_spec((d, ffn_hidden)),
                   _const_spec((ffn_hidden, d))]

    def ffn_weights(layer):
        return (vec(norm_ffn_g[layer]), ffn_w_gate[layer].astype(BF16), ffn_w_up[layer].astype(BF16),
                ffn_w_down[layer].astype(BF16))

    lam_init = 0.8 - 0.6 * math.exp(-0.3 * 0)
    cos, sa, sb = _rope_tables(n_ctx, n_lat, DIFF_HEAD_DIM, 0)
    grp = jnp.arange(COL_BLOCK) // DIFF_HEAD_DIM
    block_diag = (grp[:, None] == grp[None, :]).astype(BF16)
    qkv_shape = jax.ShapeDtypeStruct((n_rows, DIFF_WIDTH), BF16)
    u, q, k, vt = pl.pallas_call(
        _l0_proj_kernel,
        grid=(n_batch, ppb),
        in_specs=x_specs + [ctx_spec, pb_mod(0), pb_cmod(0), _const_spec((1, d)), _const_spec(ab_w_in.shape[1:]),
                    _const_spec((1, LANES)), _const_spec((1, LANES)),
                    _const_spec((COL_BLOCK, COL_BLOCK)), pb_tab, pb_tab, pb_tab],
        out_specs=[pb_row(POOL_WIDTH), pb_row(DIFF_WIDTH), pb_row(DIFF_WIDTH),
                   pl.BlockSpec((1, DIFF_WIDTH, tp), lambda b, j: (b, 0, j))],
        out_shape=[jax.ShapeDtypeStruct((n_rows, POOL_WIDTH), BF16), qkv_shape, qkv_shape,
                   jax.ShapeDtypeStruct((n_batch, DIFF_WIDTH, n_tok), BF16)],
        compiler_params=_params("arbitrary", "arbitrary"),
        name="l0_proj",
    )(*([x] * PROJ_SUBTILES), ctx, mods, mods, vec(norm_mix_g[0]), ab_w_in[0].astype(BF16),
      vec(jnp.tile(diff_q_norm_g[0], LANES // DIFF_HEAD_DIM)), vec(jnp.tile(diff_k_norm_g[0], LANES // DIFF_HEAD_DIM)),
      block_diag, cos, sa, sb)

    dw = DIFF_HEADS_PER_STEP * DIFF_PAIR
    k_spec = pl.BlockSpec((1, n_tok, dw), lambda b, hd, j: (b, 0, hd))
    v_spec = pl.BlockSpec((1, dw, n_tok), lambda b, hd, j: (b, hd, 0))
    lam_spec = pl.BlockSpec((1, DIFF_HEAD_DIM), lambda b, hd, j: (0, 0))
    dq = DIFF_QTILES_PER_STEP
    assert lpb % dq == 0
    q_specs = [pl.BlockSpec((tm, dw), lambda b, hd, j, t=t: (b * tpb + jnp.maximum(dq * (j - 1) + 1, 0) + t, hd))
               for t in range(dq)]
    o_ctx, o_lat = pl.pallas_call(
        functools.partial(_diff_attn_kernel, n_ctx=n_ctx, lam_init=lam_init),
        grid=(n_batch, DIFF_HEADS // DIFF_HEADS_PER_STEP, 1 + lpb // dq),
        in_specs=q_specs + [k_spec, v_spec, lam_spec, lam_spec, lam_spec, lam_spec,
                            pl.BlockSpec((1, DIFF_PAIR), lambda b, hd, j: (0, 0))],
        out_specs=[pl.BlockSpec((tm, dw), lambda b, hd, j: (b, hd)),
                   pl.BlockSpec((dq * tm, dw), lambda b, hd, j: (b * (lpb // dq) + jnp.maximum(j - 1, 0), hd))],
        out_shape=[jax.ShapeDtypeStruct((n_batch * n_ctx, DIFF_WIDTH), BF16),
                   jax.ShapeDtypeStruct((n_batch * n_lat, DIFF_WIDTH), BF16)],
        compiler_params=_params("arbitrary", "arbitrary", "arbitrary"),
        name="diff_attn",
    )(*([q] * dq), k.reshape(n_batch, n_tok, DIFF_WIDTH), vt,
      vec(diff_lam_q1[0]), vec(diff_lam_k1[0]), vec(diff_lam_q2[0]), vec(diff_lam_k2[0]), vec(diff_subln_g[0]))

    h = pl.pallas_call(
        functools.partial(_l0_mix_ffn_kernel, n_ctx=n_ctx, n_tok=n_tok),
        grid=(n_batch, ppb),
        in_specs=x_specs + o_specs + [ctx_spec, pl.BlockSpec((tm, DIFF_WIDTH), lambda b, j: (b, 0)),
                                      pl.BlockSpec((1, n_tok, POOL_WIDTH), lambda b, j: (b, 0, 0)),
                            _const_spec(pool_w.shape[1:]), _const_spec((1, POOL_WIDTH)),
                            _const_spec(ab_w_out.shape[1:]), pb_mod(0), pb_cmod(0)] + ffn_w_specs,
        out_specs=pb_row(d),
        out_shape=jax.ShapeDtypeStruct((n_rows, d), F32),
        compiler_params=_params("arbitrary", "arbitrary"),
        name="l0_mix_ffn",
    )(*([x] * PROJ_SUBTILES), *([o_lat] * PROJ_SUBTILES), ctx, o_ctx, u.reshape(n_batch, n_tok, POOL_WIDTH),
      pool_w[0].astype(BF16),
      vec(pool_scale[0]), ab_w_out[0].astype(BF16), mods, mods, *ffn_weights(0))

    cos, sa, sb = _rope_tables(n_ctx, n_lat, MLA_ROPE, MLA_NOPE)
    w_uq = _pad_heads(mla_w_uq[0], MLA_QK).astype(BF16)
    w_dkv = mla_w_dkv[0]
    w_dkv = jnp.concatenate([
        w_dkv[:, :MLA_KV_RANK], jnp.zeros((d, MLA_NOPE), F32), w_dkv[:, MLA_KV_RANK:],
        jnp.zeros((d, LANES - MLA_QK), F32)], axis=-1).astype(BF16)
    w_ukv = mla_w_ukv[0].reshape(MLA_KV_RANK, MLA_HEADS, MLA_NOPE + MLA_V)
    w_ukv = jnp.concatenate([
        _pad_heads(w_ukv[:, :, :MLA_NOPE].reshape(MLA_KV_RANK, -1), MLA_NOPE),
        w_ukv[:, :, MLA_NOPE:].reshape(MLA_KV_RANK, -1)], axis=-1).astype(BF16)
    pad_gain = lambda g: vec(jnp.pad(g, (0, LANES - MLA_QK)))
    head_of = jnp.arange(2 * LANES) // LANES
    head_ones = (head_of[:, None] == head_of[None, :]).astype(BF16)
    hq = MLA_HEADS * LANES
    q, k, vt = pl.pallas_call(
        _l1_proj_kernel,
        grid=(n_batch, ppb),
        in_specs=[pb_row(d), pb_mod(1), pb_cmod(1), _const_spec((1, d)), _const_spec(mla_w_dq.shape[1:]),
                  _const_spec((1, mla_w_dq.shape[2])), _const_spec(w_uq.shape), _const_spec(w_dkv.shape),
                  _const_spec((1, MLA_KV_RANK)), _const_spec(w_ukv.shape), _const_spec((1, LANES)),
                  _const_spec((1, LANES)), pb_tab, pb_tab, pb_tab, _const_spec((2 * LANES, 2 * LANES))],
        out_specs=[pb_row(hq), pb_row(hq), pl.BlockSpec((1, MLA_HEADS * MLA_V, tp), lambda b, j: (b, 0, j))],
        out_shape=[jax.ShapeDtypeStruct((n_rows, hq), BF16), jax.ShapeDtypeStruct((n_rows, hq), BF16),
                   jax.ShapeDtypeStruct((n_batch, MLA_HEADS * MLA_V, n_tok), BF16)],
        compiler_params=_params("arbitrary", "arbitrary"),
        name="l1_proj",
    )(h, mods, mods, vec(norm_mix_g[1]), mla_w_dq[0].astype(BF16), vec(mla_q_lat_g[0]), w_uq, w_dkv,
      vec(mla_kv_lat_g[0]), w_ukv, pad_gain(mla_q_norm_g[0]), pad_gain(mla_k_norm_g[0]), cos, sa, sb,
      head_ones)

    n_lat_rows = n_batch * n_lat
    qpt = MLA_QTILES_PER_STEP
    assert lpb % qpt == 0
    o = pl.pallas_call(
        _mla_attn_kernel,
        grid=(n_batch, MLA_HEADS // MLA_HEADS_PER_STEP, lpb // qpt),
        in_specs=[pl.BlockSpec((tm, MLA_HEADS_PER_STEP * LANES), lambda b, hp, j, t=t: (b * tpb + 1 + qpt * j + t, hp))
                  for t in range(qpt)]
                 + [pl.BlockSpec((1, n_tok, MLA_HEADS_PER_STEP * LANES), lambda b, hp, j: (b, 0, hp)),
                    pl.BlockSpec((1, MLA_HEADS_PER_STEP * MLA_V, n_tok), lambda b, hp, j: (b, hp, 0))],
        out_specs=pl.BlockSpec((qpt * tm, MLA_HEADS_PER_STEP * MLA_V), lambda b, hp, j: (b * (lpb // qpt) + j, hp)),
        out_shape=jax.ShapeDtypeStruct((n_lat_rows, MLA_HEADS * MLA_V), BF16),
        compiler_params=_params("arbitrary", "arbitrary", "arbitrary"),
        name="mla_attn",
    )(*([q] * qpt), k.reshape(n_batch, n_tok, hq), vt)

    h = pl.pallas_call(
        _l1_mix_ffn_kernel,
        grid=(n_lat_rows // tf,),
        in_specs=[pl.BlockSpec((tm, d), lambda t, s=s: (lat_row(FFN_SUBTILES * t + s), 0)) for s in range(FFN_SUBTILES)]
                 + [pl.BlockSpec((tf, MLA_HEADS * MLA_V), lambda t: (t, 0)), _const_spec(mla_w_out.shape[1:]),
                    pl.BlockSpec((1, 1, 6, d), lambda t: (1, t * tf // n_lat, 0, 0))] + ffn_w_specs,
        out_specs=pl.BlockSpec((tf, d), lambda t: (t, 0)),
        out_shape=jax.ShapeDtypeStruct((n_lat_rows, d), F32),
        compiler_params=_params("arbitrary", fuse_inputs=[False] * (FFN_SUBTILES + 1) + [True, False, False, True, True, True]),
        name="l1_mix_ffn",
    )(*([h] * FFN_SUBTILES), o, mla_w_out[0].astype(BF16), mods, *ffn_weights(1))
    return h.reshape(n_batch, n_lat, d)
```
